```python
import math
import jax, jax.numpy as jnp
from jax import lax
import numpy as np

D_MODEL = 1024
BATCH = 8
SEQ = 2048
DEPTH = 1
DEC_BATCH = 32
DEC_SEQ = 8
PAST_LEN = 16384
PAGE_SIZE = 128

N_HEADS_A = 12
HEAD_DIM_A = 64
D_ATTN = N_HEADS_A * HEAD_DIM_A
DILATED_GROUPS = ((128, 1), (512, 4), (2048, 16))
WINDOW_MAX = 2048
QUERY_BLOCK = 64
N_BUCKETS = 32
MAX_DISTANCE = WINDOW_MAX
D_SSM = 2 * D_MODEL
SSM_HEAD_DIM = 64
N_SSM_HEADS = D_SSM // SSM_HEAD_DIM
SSM_GROUPS = 8
D_STATE = 128
CONV_WIDTH = 4
D_CONV = D_SSM + 2 * SSM_GROUPS * D_STATE
SSD_CHUNK = 128
DT_MIN = 0.001
DT_MAX = 0.1
SPLIT_SIZES = (D_ATTN, D_ATTN, D_ATTN, D_ATTN, D_SSM, D_CONV, N_SSM_HEADS, D_MODEL, D_MODEL)
D_IN_PROJ = sum(SPLIT_SIZES)
SPLIT_POINTS = tuple(int(s) for s in np.cumsum(SPLIT_SIZES)[:-1])

kernel_name = 'hybrid_dilated_attn_mamba2_gated_decode_step'


def _rmsnorm(x, g, eps=1e-6):
    xf = x.astype(jnp.float32)
    y = xf * lax.rsqrt(jnp.mean(xf * xf, axis=-1, keepdims=True) + eps)
    return (y * g.astype(jnp.float32)).astype(x.dtype)


def _group_rmsnorm(y, g, eps=1e-5):
    shp = y.shape
    yg = y.reshape(shp[:-1] + (SSM_GROUPS, shp[-1] // SSM_GROUPS))
    yg = yg * lax.rsqrt(jnp.mean(yg * yg, axis=-1, keepdims=True) + eps)
    return yg.reshape(shp) * g.astype(jnp.float32)


def _t5_bucket(dist):
    max_exact = N_BUCKETS // 2
    d = np.maximum(dist, 1).astype(np.float32)
    large = max_exact + (np.log(d / max_exact) / math.log(MAX_DISTANCE / max_exact)
                         * (N_BUCKETS - max_exact)).astype(np.int32)
    large = np.minimum(large, N_BUCKETS - 1)
    return np.where(dist < max_exact, dist, large).astype(np.int32)


def _dilated_attention(q, k_all, v_all, rel_bias, q_pos0, k_pos0):
    bsz, lq, n_heads, head_dim = q.shape
    lk = k_all.shape[1]
    qb = QUERY_BLOCK if lq % QUERY_BLOCK == 0 else lq
    n_blocks = lq // qb
    scale = 1.0 / math.sqrt(head_dim)
    offsets = [np.arange(0, w + 1, d, dtype=np.int32) for (w, d) in DILATED_GROUPS]
    biases = [rel_bias[_t5_bucket(o)].T.astype(jnp.float32) for o in offsets]

    def one_block(blk):
        start = blk * qb
        qs = lax.dynamic_slice_in_dim(q, start, qb, axis=1).astype(jnp.float32) * scale
        qpos = q_pos0 + start + jnp.arange(qb, dtype=jnp.int32)
        lses, outs = [], []
        for off, bias in zip(offsets, biases):
            kpos = qpos[:, None] - off[None, :]
            idx = kpos - k_pos0
            valid = (kpos >= 0) & (idx >= 0)
            idx = jnp.clip(idx, 0, lk - 1)
            kg = jnp.take(k_all, idx, axis=1).astype(jnp.float32)
            vg = jnp.take(v_all, idx, axis=1).astype(jnp.float32)
            s = jnp.einsum('bqhe,bqjhe->bhqj', qs, kg) + bias[None, :, None, :]
            s = jnp.where(valid[None, None], s, -jnp.inf)
            lse = jax.nn.logsumexp(s, axis=-1)
            p = jnp.exp(s - lse[..., None])
            outs.append(jnp.einsum('bhqj,bqjhe->bqhe', p, vg))
            lses.append(lse)
        mix = jax.nn.softmax(jnp.stack(lses, axis=0), axis=0)
        o = jnp.einsum('gbhq,gbqhe->bqhe', mix, jnp.stack(outs, axis=0))
        return o.astype(q.dtype)

    out = lax.map(one_block, jnp.arange(n_blocks))
    return jnp.moveaxis(out, 0, 1).reshape(bsz, lq, n_heads, head_dim)


def _causal_dwconv(xbc, conv_past, conv_w, conv_b):
    xp = jnp.concatenate([conv_past.astype(xbc.dtype), xbc], axis=1)
    length = xbc.shape[1]
    y = conv_b
    for tap in range(CONV_WIDTH):
        y = y + xp[:, tap:tap + length] * conv_w[tap]
    return y, xp[:, -(CONV_WIDTH - 1):]


def _ssd(x, dt, a, bm, cm, h0):
    f32 = jnp.float32
    bsz, length, n_heads, p_dim = x.shape
    n_groups, n_state = bm.shape[2], bm.shape[3]
    r = n_heads // n_groups
    qc = SSD_CHUNK if length % SSD_CHUNK == 0 else length
    nc = length // qc
    xc = x.astype(f32).reshape(bsz, nc, qc, n_groups, r, p_dim)
    dtc = dt.astype(f32).reshape(bsz, nc, qc, n_groups, r)
    bc = bm.astype(f32).reshape(bsz, nc, qc, n_groups, n_state)
    cc = cm.astype(f32).reshape(bsz, nc, qc, n_groups, n_state)
    acum = jnp.cumsum(dtc * a.astype(f32).reshape(n_groups, r), axis=2)
    xdt = xc * dtc[..., None]
    causal = np.tril(np.ones((qc, qc), dtype=bool))[:, :, None, None]
    seg = acum[:, :, :, None] - acum[:, :, None, :]
    decay = jnp.exp(jnp.where(causal, seg, -jnp.inf))
    cb = jnp.einsum('bclgn,bcsgn->bclsg', cc, bc)
    y_diag = jnp.einsum('bclsgr,bcsgrp->bclgrp', cb[..., None] * decay, xdt)
    wx = xdt * jnp.exp(acum[:, :, -1:] - acum)[..., None]
    states = jnp.einsum('bclgn,bclgrp->bcgrpn', bc, wx)
    chunk_decay = jnp.exp(acum[:, :, -1])

    def step(h, inp):
        s_c, d_c = inp
        return d_c[..., None, None] * h + s_c, h

    h_init = h0.astype(f32).reshape(bsz, n_groups, r, p_dim, n_state)
    h_final, h_prev = lax.scan(step, h_init, (jnp.moveaxis(states, 1, 0), jnp.moveaxis(chunk_decay, 1, 0)))
    h_prev = jnp.moveaxis(h_prev, 0, 1)
    y_off = jnp.einsum('bclgn,bcgrpn->bclgrp', cc, h_prev) * jnp.exp(acum)[..., None]
    y = (y_diag + y_off).reshape(bsz, length, n_heads, p_dim)
    return y, h_final.reshape(bsz, n_heads, p_dim, n_state)


def _layer(x, k_past, v_past, conv_past, ssm_past, pos0,
           norm_g, w_in, conv_w, conv_b, dt_bias, a_log, d_skip, ssm_norm,
           w_branch_a, w_branch_b, w_out, rel_bias):
    f32 = jnp.float32
    bsz, length, _ = x.shape
    h = _rmsnorm(x, norm_g)
    proj = jnp.einsum('bld,dn->bln', h, w_in)
    q, k, v, g_attn, z, xbc, dt_raw, gate_a, gate_b = jnp.split(proj, SPLIT_POINTS, axis=-1)

    q = q.reshape(bsz, length, N_HEADS_A, HEAD_DIM_A)
    k = k.reshape(bsz, length, N_HEADS_A, HEAD_DIM_A)
    v = v.reshape(bsz, length, N_HEADS_A, HEAD_DIM_A)
    k_all = jnp.concatenate([k_past.astype(k.dtype), k], axis=1)
    v_all = jnp.concatenate([v_past.astype(v.dtype), v], axis=1)
    k_pos0 = pos0 - k_past.shape[1]
    y_a = _dilated_attention(q, k_all, v_all, rel_bias, pos0, k_pos0).reshape(bsz, length, D_ATTN)
    y_a = y_a * jax.nn.silu(g_attn)

    xbc_c, conv_new = _causal_dwconv(xbc, conv_past, conv_w, conv_b)
    xbc_c = jax.nn.silu(xbc_c)
    xs, bm, cm = jnp.split(xbc_c, [D_SSM, D_SSM + SSM_GROUPS * D_STATE], axis=-1)
    xs = xs.reshape(bsz, length, N_SSM_HEADS, SSM_HEAD_DIM)
    bm = bm.reshape(bsz, length, SSM_GROUPS, D_STATE)
    cm = cm.reshape(bsz, length, SSM_GROUPS, D_STATE)
    dt = jax.nn.softplus(dt_raw.astype(f32) + dt_bias.astype(f32))
    a = -jnp.exp(a_log.astype(f32))
    y_s, ssm_new = _ssd(xs, dt, a, bm, cm, ssm_past)
    y_s = y_s + d_skip.astype(f32)[:, None] * xs.astype(f32)
    y_s = y_s.reshape(bsz, length, D_SSM) * jax.nn.silu(z.astype(f32))
    y_s = _group_rmsnorm(y_s, ssm_norm).astype(x.dtype)

    merged = (jax.nn.sigmoid(gate_a) * jnp.einsum('blc,cd->bld', y_a, w_branch_a)
              + jax.nn.sigmoid(gate_b) * jnp.einsum('blc,cd->bld', y_s, w_branch_b))
    out = x + jnp.einsum('bld,de->ble', merged, w_out)
    return out, k, v, conv_new, ssm_new


def setup_inputs(seed: int = 0) -> dict:
    key = jax.random.key(seed)
    ks = jax.random.split(key, 20)
    f32 = jnp.float32
    kv_buf = min(WINDOW_MAX, PAST_LEN)

    def nrm(k, shape, s):
        return jax.random.normal(k, shape, f32) * s

    dt0 = jnp.exp(jax.random.uniform(ks[11], (DEPTH, N_SSM_HEADS), f32, math.log(DT_MIN), math.log(DT_MAX)))
    return {
        'x_prompt': nrm(ks[0], (BATCH, SEQ, D_MODEL), 1.0),
        'x_sample': nrm(ks[1], (DEC_BATCH, DEC_SEQ, D_MODEL), 1.0),
        'cache_k': nrm(ks[2], (DEPTH, DEC_BATCH, kv_buf, N_HEADS_A, HEAD_DIM_A), 1.0),
        'cache_v': nrm(ks[3], (DEPTH, DEC_BATCH, kv_buf, N_HEADS_A, HEAD_DIM_A), 1.0),
        'state_conv': nrm(ks[4], (DEPTH, DEC_BATCH, CONV_WIDTH - 1, D_CONV), 1.0),
        'state_ssm': nrm(ks[5], (DEPTH, DEC_BATCH, N_SSM_HEADS, SSM_HEAD_DIM, D_STATE), 0.3),
        'norm_g': 1.0 + nrm(ks[6], (DEPTH, D_MODEL), 0.05),
        'w_in': nrm(ks[7], (DEPTH, D_MODEL, D_IN_PROJ), D_MODEL ** -0.5),
        'conv_w': nrm(ks[8], (DEPTH, CONV_WIDTH, D_CONV), CONV_WIDTH ** -0.5),
        'conv_b': nrm(ks[9], (DEPTH, D_CONV), 0.02),
        'dt_bias': dt0 + jnp.log(-jnp.expm1(-dt0)),
        'a_log': jnp.log(jax.random.uniform(ks[10], (DEPTH, N_SSM_HEADS), f32, 1.0, 16.0)),
        'd_skip': 1.0 + nrm(ks[12], (DEPTH, N_SSM_HEADS), 0.1),
        'ssm_norm': 1.0 + nrm(ks[13], (DEPTH, D_SSM), 0.05),
        'w_branch_a': nrm(ks[14], (DEPTH, D_ATTN, D_MODEL), D_ATTN ** -0.5),
        'w_branch_b': nrm(ks[15], (DEPTH, D_SSM, D_MODEL), D_SSM ** -0.5),
        'w_out': nrm(ks[16], (DEPTH, D_MODEL, D_MODEL), D_MODEL ** -0.5),
        'rel_bias': nrm(ks[17], (N_BUCKETS, N_HEADS_A), 0.5),
        'final_norm': 1.0 + nrm(ks[18], (D_MODEL,), 0.05),
    }


def reference(x_prompt, x_sample, cache_k, cache_v, state_conv, state_ssm,
              norm_g, w_in, conv_w, conv_b, dt_bias, a_log, d_skip, ssm_norm,
              w_branch_a, w_branch_b, w_out, rel_bias, final_norm):
    bp, lp = x_prompt.shape[0], x_prompt.shape[1]
    keep = min(WINDOW_MAX, lp)
    yp, ys = x_prompt, x_sample
    nk_p, nv_p, nc_p, ns_p = [], [], [], []
    nk_s, nv_s, nc_s, ns_s = [], [], [], []
    for layer in range(DEPTH):
        wts = (norm_g[layer], w_in[layer], conv_w[layer], conv_b[layer], dt_bias[layer],
               a_log[layer], d_skip[layer], ssm_norm[layer], w_branch_a[layer],
               w_branch_b[layer], w_out[layer], rel_bias)
        k0 = jnp.zeros((bp, 0, N_HEADS_A, HEAD_DIM_A), x_prompt.dtype)
        c0 = jnp.zeros((bp, CONV_WIDTH - 1, D_CONV), x_prompt.dtype)
        s0 = jnp.zeros((bp, N_SSM_HEADS, SSM_HEAD_DIM, D_STATE), jnp.float32)
        yp, kp, vp, cp, sp = _layer(yp, k0, k0, c0, s0, 0, *wts)
        nk_p.append(kp[:, lp - keep:])
        nv_p.append(vp[:, lp - keep:])
        nc_p.append(cp)
        ns_p.append(sp)
        ys, ks_, vs_, cs_, ss_ = _layer(ys, cache_k[layer], cache_v[layer], state_conv[layer],
                                        state_ssm[layer], PAST_LEN, *wts)
        nk_s.append(ks_)
        nv_s.append(vs_)
        nc_s.append(cs_)
        ns_s.append(ss_)
    y_prompt = _rmsnorm(yp, final_norm)
    y_sample = _rmsnorm(ys, final_norm)
    return (y_prompt, y_sample,
            jnp.stack(nk_p), jnp.stack(nv_p), jnp.stack(nc_p), jnp.stack(ns_p),
            jnp.stack(nk_s), jnp.stack(nv_s), jnp.stack(nc_s), jnp.stack(ns_s))
```

```python
import functools
import math

import numpy as np
import jax
import jax.numpy as jnp
from jax import lax
from jax.experimental import pallas as pl
from jax.experimental.pallas import tpu as pltpu

F32 = jnp.float32
BF16 = jnp.bfloat16

D_MODEL = 1024
N_HEADS_A = 12
HEAD_DIM_A = 64
D_ATTN = N_HEADS_A * HEAD_DIM_A
DILATIONS = (1, 4, 16)
WINDOW_KEYS = 128
WINDOW_MAX = 2048
N_BUCKETS = 32
MAX_DISTANCE = WINDOW_MAX
D_SSM = 2048
SSM_HEAD_DIM = 64
N_SSM_HEADS = D_SSM // SSM_HEAD_DIM
SSM_GROUPS = 8
HEADS_PER_GROUP = N_SSM_HEADS // SSM_GROUPS
GROUP_WIDTH = D_SSM // SSM_GROUPS
D_STATE = 128
CONV_WIDTH = 4
D_BC = SSM_GROUPS * D_STATE
D_CONV = D_SSM + 2 * D_BC
CHUNK = 128
LANES = 128
SUBLANES = 8
NEG = -1e30
VMEM_LIMIT = 48 * 1024 * 1024


def _cparams(*sem):
    return pltpu.CompilerParams(dimension_semantics=sem, vmem_limit_bytes=VMEM_LIMIT)


def _dot(a, b):
    return jnp.dot(a, b, preferred_element_type=F32)


def _dot_nt(a, b):
    return lax.dot_general(a, b, (((1,), (1,)), ((), ())), preferred_element_type=F32)


def _silu(x):
    return x * (1.0 / (1.0 + jnp.exp(-x)))


def _sigmoid(x):
    return 1.0 / (1.0 + jnp.exp(-x))


def _div_pow2(x, n):
    assert n & (n - 1) == 0
    return jnp.right_shift(x, int(math.log2(n)))


def _expand_lanes(w, e):
    hi = w.astype(BF16)
    lo = (w - hi.astype(F32)).astype(BF16)
    return _dot(hi, e) + _dot(lo, e)


def _proj_a_kernel(x_ref, g_ref, w_ref, h_ref, qb_ref, k_ref, v_ref, kb_ref, vb_ref, ga_ref):
    xf = x_ref[...]
    ms = jnp.mean(xf * xf, axis=-1, keepdims=True)
    h = (xf * lax.rsqrt(ms + 1e-6) * g_ref[...]).astype(BF16)
    h_ref[...] = h
    q = _dot(h, w_ref[:, 0:D_ATTN])
    qb_ref[...] = (q * (1.0 / math.sqrt(HEAD_DIM_A))).astype(BF16)
    k = _dot(h, w_ref[:, D_ATTN:2 * D_ATTN])
    k_ref[...] = k
    kb_ref[...] = k.astype(BF16)
    v = _dot(h, w_ref[:, 2 * D_ATTN:3 * D_ATTN])
    v_ref[...] = v
    vb_ref[...] = v.astype(BF16)
    ga_ref[...] = _dot(h, w_ref[:, 3 * D_ATTN:4 * D_ATTN])


def _proj_kernel(h_ref, w_ref, *out_refs, widths):
    h = h_ref[...]
    c0 = 0
    for o_ref, wd in zip(out_refs, widths):
        o_ref[...] = _dot(h, w_ref[:, c0:c0 + wd])
        c0 += wd


def _row_tile(t):
    return 512 if t % 512 == 0 else t


def _proj_a(x, norm_g, w_a):
    t = x.shape[0]
    tm = _row_tile(t)
    row = lambda n: pl.BlockSpec((tm, n), lambda i: (i, 0))
    full = lambda a: pl.BlockSpec(a.shape, lambda i: (0, 0))
    outs = [
        jax.ShapeDtypeStruct((t, D_MODEL), BF16),
        jax.ShapeDtypeStruct((t, D_ATTN), BF16),
        jax.ShapeDtypeStruct((t, D_ATTN), F32),
        jax.ShapeDtypeStruct((t, D_ATTN), F32),
        jax.ShapeDtypeStruct((t, D_ATTN), BF16),
        jax.ShapeDtypeStruct((t, D_ATTN), BF16),
        jax.ShapeDtypeStruct((t, D_ATTN), F32),
    ]
    return pl.pallas_call(
        _proj_a_kernel,
        grid=(t // tm,),
        in_specs=[row(D_MODEL), full(norm_g), full(w_a)],
        out_specs=[row(s.shape[1]) for s in outs],
        out_shape=outs,
        compiler_params=_cparams("parallel"),
        name="proj_a",
    )(x, norm_g, w_a)


def _proj(h, w, widths, name):
    t = h.shape[0]
    tm = _row_tile(t)
    row = lambda n: pl.BlockSpec((tm, n), lambda i: (i, 0))
    outs = [jax.ShapeDtypeStruct((t, wd), F32) for wd in widths]
    return pl.pallas_call(
        functools.partial(_proj_kernel, widths=widths),
        grid=(t // tm,),
        in_specs=[row(D_MODEL), pl.BlockSpec(w.shape, lambda i: (0, 0))],
        out_specs=[row(wd) for wd in widths],
        out_shape=outs,
        compiler_params=_cparams("parallel"),
        name=name,
    )(h, w)


def _attn_group_kernel(q_ref, k_ref, v_ref, bias_ref, o_ref, lse_ref, *, lc):
    nblk = lc // CHUNK
    lane = lax.broadcasted_iota(jnp.int32, (CHUNK, LANES), 1)
    first_half = lane < HEAD_DIM_A

    def block(i, has_prev):
        if has_prev:
            q0 = pl.multiple_of(i * CHUNK, CHUNK)
            p0 = pl.multiple_of(jnp.maximum(i - 1, 0) * CHUNK, CHUNK)
            pen = jnp.where(i == 0, NEG, 0.0)
        else:
            q0 = i * CHUNK
        lse_tile = jnp.zeros((CHUNK, LANES), F32)
        for p in range(N_HEADS_A // 2):
            cs = slice(LANES * p, LANES * (p + 1))
            qp = q_ref[0, pl.ds(q0, CHUNK), cs].astype(F32)
            qs = jnp.concatenate([jnp.where(first_half, qp, 0.0),
                                  jnp.where(first_half, 0.0, qp)], axis=0).astype(BF16)
            kc = k_ref[0, pl.ds(q0, CHUNK), cs]
            vc = v_ref[0, pl.ds(q0, CHUNK), cs]
            if has_prev:
                kw = jnp.concatenate([k_ref[0, pl.ds(p0, CHUNK), cs], kc], axis=0)
                vw = jnp.concatenate([v_ref[0, pl.ds(p0, CHUNK), cs], vc], axis=0)
                s = _dot_nt(qs, kw) + bias_ref[p]
                col = lax.broadcasted_iota(jnp.int32, s.shape, 1)
                s = s + jnp.where(col < CHUNK, pen, 0.0)
            else:
                kw, vw = kc, vc
                s = _dot_nt(qs, kw) + bias_ref[p, :, CHUNK:]
            m = jnp.max(s, axis=1, keepdims=True)
            e = jnp.exp(s - m)
            l = jnp.sum(e, axis=1, keepdims=True)
            pv = _dot(e.astype(BF16), vw) * (1.0 / l)
            o_ref[0, pl.ds(q0, CHUNK), cs] = jnp.where(first_half, pv[:CHUNK], pv[CHUNK:])
            lse = m + jnp.log(l)
            lse_tile = jnp.where(lane == 2 * p, lse[:CHUNK],
                                 jnp.where(lane == 2 * p + 1, lse[CHUNK:], lse_tile))
        lse_ref[0, pl.ds(q0, CHUNK), :] = lse_tile

    if nblk == 1:
        block(0, False)
    else:
        def body(i, carry):
            block(i, True)
            return carry
        lax.fori_loop(0, nblk, body, 0)


def _attn_group(qb, kb, vb, bias, bsz, seq, d):
    lc = seq // d
    view = lambda a: a.reshape(bsz, lc, d * D_ATTN)
    cls = pl.BlockSpec((1, lc, D_ATTN), lambda b, r: (b, 0, r))
    o, lse = pl.pallas_call(
        functools.partial(_attn_group_kernel, lc=lc),
        grid=(bsz, d),
        in_specs=[cls, cls, cls, pl.BlockSpec(bias.shape, lambda b, r: (0, 0, 0))],
        out_specs=[cls, pl.BlockSpec((1, lc, LANES), lambda b, r: (b, 0, r))],
        out_shape=[jax.ShapeDtypeStruct((bsz, lc, d * D_ATTN), F32),
                   jax.ShapeDtypeStruct((bsz, lc, d * LANES), F32)],
        compiler_params=_cparams("parallel", "parallel"),
        name=f"attn_d{d}",
    )(view(qb), view(kb), view(vb), bias)
    return o.reshape(bsz * seq, D_ATTN), lse.reshape(bsz * seq, LANES)


def _t5_bucket(dist):
    max_exact = N_BUCKETS // 2
    d = np.maximum(dist, 1).astype(np.float32)
    large = max_exact + (np.log(d / max_exact) / math.log(MAX_DISTANCE / max_exact)
                         * (N_BUCKETS - max_exact)).astype(np.int32)
    large = np.minimum(large, N_BUCKETS - 1)
    return np.where(dist < max_exact, dist, large).astype(np.int32)


def _window_bias(rel_bias, d):
    qi = np.arange(CHUNK)[:, None]
    kj = np.arange(2 * CHUNK)[None, :]
    rel = qi + CHUNK - kj
    valid = (rel >= 0) & (rel <= WINDOW_KEYS)
    bucket = _t5_bucket(np.clip(rel, 0, WINDOW_KEYS) * d)
    b = jnp.where(valid[..., None], rel_bias.astype(F32)[bucket], NEG)
    return jnp.transpose(b, (2, 0, 1)).reshape(N_HEADS_A // 2, 2 * CHUNK, 2 * CHUNK)


def _attn_sample_kernel(q_ref, kn_ref, vn_ref, ck_ref, cv_ref, bias_ref, o_ref, kall_ref, vall_ref,
                        *, lq, buf):
    nrow = N_HEADS_A * lq
    pad = jnp.zeros((CHUNK - lq, D_ATTN), F32)
    kall_ref[0:buf, :] = ck_ref[0].astype(BF16)
    vall_ref[0:buf, :] = cv_ref[0].astype(BF16)
    kall_ref[buf:buf + CHUNK, :] = jnp.concatenate([kn_ref[0], pad], axis=0).astype(BF16)
    vall_ref[buf:buf + CHUNK, :] = jnp.concatenate([vn_ref[0], pad], axis=0).astype(BF16)
    q = q_ref[0].astype(F32)
    row_head = _div_pow2(lax.broadcasted_iota(jnp.int32, (nrow, D_ATTN), 0), lq)
    col_head = _div_pow2(lax.broadcasted_iota(jnp.int32, (nrow, D_ATTN), 1), HEAD_DIM_A)
    own = row_head == col_head
    qs = jnp.where(own, jnp.concatenate([q] * N_HEADS_A, axis=0), 0.0).astype(BF16)
    s = _dot_nt(qs, kall_ref[...]) + bias_ref[...]
    m = jnp.max(s, axis=1, keepdims=True)
    e = jnp.exp(s - m)
    l = jnp.sum(e, axis=1, keepdims=True)
    pv = jnp.where(own, _dot(e.astype(BF16), vall_ref[...]) * (1.0 / l), 0.0)
    o = pv[0:lq]
    for h in range(1, N_HEADS_A):
        o = o + pv[h * lq:(h + 1) * lq]
    o_ref[0] = o


def _sample_bias(rel_bias, lq, buf):
    i = np.arange(lq)[:, None]
    c = np.arange(buf + CHUNK)[None, :]
    dist = np.where(c < buf, buf + i - c, i - (c - buf))
    real = (c < buf + lq) & (dist >= 0)
    mult = np.zeros(dist.shape, np.int32)
    for d in DILATIONS:
        mult += (real & (dist % d == 0) & (dist <= WINDOW_KEYS * d)).astype(np.int32)
    bucket = _t5_bucket(np.clip(dist, 0, WINDOW_MAX))
    logm = np.log(np.maximum(mult, 1)).astype(np.float32)
    b = rel_bias.astype(F32)[bucket] + logm[..., None]
    b = jnp.where((mult > 0)[..., None], b, NEG)
    return jnp.transpose(b, (2, 0, 1)).reshape(N_HEADS_A * lq, buf + CHUNK)


def _attn_sample(qb, k_new, v_new, cache_k, cache_v, bias, bsz, lq):
    buf = cache_k.shape[1]
    assert lq & (lq - 1) == 0 and lq <= CHUNK
    tok = pl.BlockSpec((1, lq, D_ATTN), lambda b: (b, 0, 0))
    cache = pl.BlockSpec((1, buf, D_ATTN), lambda b: (b, 0, 0))
    per_seq = lambda a: a.reshape(bsz, lq, D_ATTN)
    o = pl.pallas_call(
        functools.partial(_attn_sample_kernel, lq=lq, buf=buf),
        grid=(bsz,),
        in_specs=[tok, tok, tok, cache, cache, pl.BlockSpec(bias.shape, lambda b: (0, 0))],
        out_specs=tok,
        out_shape=jax.ShapeDtypeStruct((bsz, lq, D_ATTN), F32),
        scratch_shapes=[pltpu.VMEM((buf + CHUNK, D_ATTN), BF16),
                        pltpu.VMEM((buf + CHUNK, D_ATTN), BF16)],
        compiler_params=_cparams("parallel"),
        name="attn_sample",
    )(per_seq(qb), per_seq(k_new), per_seq(v_new), cache_k, cache_v, bias)
    return o.reshape(bsz * lq, D_ATTN)


def _ssd_kernel(*refs, rows, has_state):
    (xs_ref, bm_ref, cm_ref, dt_ref, z_ref, past_ref) = refs[:6]
    refs = refs[6:]
    if has_state:
        h0_ref = refs[0]
        refs = refs[1:]
    (cw_ref, cb_ref, dtb_ref, a_ref, dsk_ref, nrm_ref, tri_ref,
     y_ref, hout_ref, xpad_ref, xc_ref) = refs
    c = pl.program_id(1)

    @pl.when(c == 0)
    def _():
        xpad_ref[0:SUBLANES, :] = past_ref[0]
        if has_state:
            hout_ref[0] = h0_ref[0]
        else:
            hout_ref[0] = jnp.zeros((D_SSM, D_STATE), F32)
        if rows < CHUNK:
            xpad_ref[SUBLANES + rows:SUBLANES + CHUNK, :] = jnp.zeros((CHUNK - rows, D_CONV), F32)

    @pl.when(c > 0)
    def _():
        xpad_ref[0:SUBLANES, :] = xpad_ref[CHUNK:CHUNK + SUBLANES, :]

    xpad_ref[SUBLANES:SUBLANES + rows, 0:D_SSM] = xs_ref[0]
    xpad_ref[SUBLANES:SUBLANES + rows, D_SSM:D_SSM + D_BC] = bm_ref[0]
    xpad_ref[SUBLANES:SUBLANES + rows, D_SSM + D_BC:D_CONV] = cm_ref[0]

    slab = 512
    for j in range(D_CONV // slab):
        cs = slice(j * slab, (j + 1) * slab)
        acc = cb_ref[:, cs]
        for tap in range(CONV_WIDTH):
            off = SUBLANES - (CONV_WIDTH - 1) + tap
            acc = acc + xpad_ref[off:off + CHUNK, cs] * cw_ref[tap:tap + 1, cs]
        xc_ref[:, cs] = _silu(acc)

    dt_raw = dt_ref[0]
    if rows < CHUNK:
        dt_raw = jnp.concatenate([dt_raw, jnp.zeros((CHUNK - rows, LANES), F32)], axis=0)
    pre = dt_raw + dtb_ref[...]
    dt = jnp.maximum(pre, 0.0) + jnp.log(1.0 + jnp.exp(-jnp.abs(pre)))
    if rows < CHUNK:
        rid = lax.broadcasted_iota(jnp.int32, (CHUNK, LANES), 0)
        dt = jnp.where(rid < rows, dt, 0.0)
    da = dt * a_ref[...]
    acum = jnp.dot(tri_ref[...], da, preferred_element_type=F32, precision=lax.Precision.HIGHEST)
    acum_t = acum.T
    last = acum[CHUNK - 1:CHUNK, :]
    e_acum = jnp.exp(acum)
    w_coef = dt * jnp.exp(last - acum)

    ri = lax.broadcasted_iota(jnp.int32, (CHUNK, CHUNK), 0)
    ci = lax.broadcasted_iota(jnp.int32, (CHUNK, CHUNK), 1)
    causal = ri >= ci
    glane = _div_pow2(lax.broadcasted_iota(jnp.int32, (CHUNK, GROUP_WIDTH), 1), SSM_HEAD_DIM)
    grow = _div_pow2(lax.broadcasted_iota(jnp.int32, (GROUP_WIDTH, D_STATE), 0), SSM_HEAD_DIM)

    def per_head_lanes(mat, g):
        out = jnp.broadcast_to(mat[:, HEADS_PER_GROUP * g:HEADS_PER_GROUP * g + 1], (CHUNK, GROUP_WIDTH))
        for k in range(1, HEADS_PER_GROUP):
            h = HEADS_PER_GROUP * g + k
            out = jnp.where(glane == k, jnp.broadcast_to(mat[:, h:h + 1], (CHUNK, GROUP_WIDTH)), out)
        return out

    for g in range(SSM_GROUPS):
        gs = slice(g * GROUP_WIDTH, (g + 1) * GROUP_WIDTH)
        bg = xc_ref[:, D_SSM + g * D_STATE:D_SSM + (g + 1) * D_STATE].astype(BF16)
        cg = xc_ref[:, D_SSM + D_BC + g * D_STATE:D_SSM + D_BC + (g + 1) * D_STATE].astype(BF16)
        xg = xc_ref[:, gs]
        cb = _dot_nt(cg, bg)
        xdt = xg * per_head_lanes(dt, g)
        m_parts, x_parts = [], []
        for k in range(HEADS_PER_GROUP):
            h = HEADS_PER_GROUP * g + k
            seg = acum[:, h:h + 1] - acum_t[h:h + 1, :]
            decay = jnp.exp(jnp.where(causal, seg, NEG))
            m_parts.append((cb * decay).astype(BF16))
            x_parts.append(jnp.where(glane == k, xdt, 0.0).astype(BF16))
        y = _dot(jnp.concatenate(m_parts, axis=1), jnp.concatenate(x_parts, axis=0))
        hg = hout_ref[0, gs, :]
        y = y + _dot_nt(cg, hg.astype(BF16)) * per_head_lanes(e_acum, g)
        y = y + dsk_ref[:, gs] * xg
        wx_t = (xg * per_head_lanes(w_coef, g)).T.astype(BF16)
        cdec = jnp.broadcast_to(last[:, HEADS_PER_GROUP * g:HEADS_PER_GROUP * g + 1], (GROUP_WIDTH, D_STATE))
        for k in range(1, HEADS_PER_GROUP):
            h = HEADS_PER_GROUP * g + k
            cdec = jnp.where(grow == k, jnp.broadcast_to(last[:, h:h + 1], (GROUP_WIDTH, D_STATE)), cdec)
        hout_ref[0, gs, :] = jnp.exp(cdec) * hg + _dot(wx_t, bg)
        yz = y[:rows] * _silu(z_ref[0, :, gs])
        ms = jnp.mean(yz * yz, axis=1, keepdims=True)
        y_ref[0, :, gs] = (yz * lax.rsqrt(ms + 1e-5) * nrm_ref[:, gs]).astype(y_ref.dtype)


def _ssd(xs, bm, cm, dt, z, past8, h0, wts, bsz, seq):
    rows = CHUNK if seq % CHUNK == 0 else seq
    nc = seq // rows
    tok = lambda n: pl.BlockSpec((1, rows, n), lambda b, c: (b * nc + c, 0, 0))
    full2 = lambda a: pl.BlockSpec(a.shape, lambda b, c: (0, 0))
    state = pl.BlockSpec((1, D_SSM, D_STATE), lambda b, c: (b, 0, 0))
    has_state = h0 is not None
    chunks = lambda a: a.reshape(bsz * nc, rows, a.shape[-1])
    ins = ([chunks(a) for a in (xs, bm, cm, dt, z)] + [past8] + ([h0] if has_state else []) + list(wts))
    in_specs = ([tok(D_SSM), tok(D_BC), tok(D_BC), tok(LANES), tok(D_SSM),
                 pl.BlockSpec((1, SUBLANES, D_CONV), lambda b, c: (b, 0, 0))]
                + ([state] if has_state else []) + [full2(w) for w in wts])
    y, h_new = pl.pallas_call(
        functools.partial(_ssd_kernel, rows=rows, has_state=has_state),
        grid=(bsz, nc),
        in_specs=in_specs,
        out_specs=[tok(D_SSM), state],
        out_shape=[jax.ShapeDtypeStruct((bsz * nc, rows, D_SSM), BF16),
                   jax.ShapeDtypeStruct((bsz, D_SSM, D_STATE), F32)],
        scratch_shapes=[pltpu.VMEM((SUBLANES + CHUNK, D_CONV), F32),
                        pltpu.VMEM((CHUNK, D_CONV), F32)],
        compiler_params=_cparams("parallel", "arbitrary"),
        name="ssd",
    )(*ins)
    return y.reshape(bsz * seq, D_SSM), h_new


def _merge_kernel(*refs, n_groups):
    o_refs = refs[:n_groups]
    lse_refs = refs[n_groups:2 * n_groups] if n_groups > 1 else ()
    refs = refs[(2 * n_groups if n_groups > 1 else 1):]
    (g_ref, ys_ref, ga_ref, gb_ref, x_ref, exp_ref, wa_ref, wb_ref, wo_ref, fn_ref, y_ref) = refs
    if n_groups > 1:
        lses = [r[...] for r in lse_refs]
        mx = functools.reduce(jnp.maximum, lses)
        es = [jnp.exp(l - mx) for l in lses]
        inv = 1.0 / functools.reduce(lambda a, b: a + b, es)
        ya = None
        for o_ref, e in zip(o_refs, es):
            term = _expand_lanes(e * inv, exp_ref[...]) * o_ref[...]
            ya = term if ya is None else ya + term
    else:
        ya = o_refs[0][...]
    ya = (ya * _silu(g_ref[...])).astype(BF16)
    merged = (_sigmoid(ga_ref[...]) * _dot(ya, wa_ref[...])
              + _sigmoid(gb_ref[...]) * _dot(ys_ref[...], wb_ref[...]))
    out = x_ref[...] + _dot(merged.astype(BF16), wo_ref[...])
    ms = jnp.mean(out * out, axis=-1, keepdims=True)
    y_ref[...] = out * lax.rsqrt(ms + 1e-6) * fn_ref[...]


def _merge(os_, lses, g_attn, ys, gate_a, gate_b, x, expand, wa, wb, wo, fnorm):
    t = x.shape[0]
    tm = _row_tile(t) // 2
    n_groups = len(os_)
    row = lambda n: pl.BlockSpec((tm, n), lambda i: (i, 0))
    full = lambda a: pl.BlockSpec(a.shape, lambda i: (0, 0))
    ins = list(os_) + list(lses) + [g_attn, ys, gate_a, gate_b, x, expand, wa, wb, wo, fnorm]
    in_specs = ([row(D_ATTN)] * n_groups + [row(LANES)] * len(lses)
                + [row(D_ATTN), row(D_SSM), row(D_MODEL), row(D_MODEL), row(D_MODEL),
                   full(expand), full(wa), full(wb), full(wo), full(fnorm)])
    return pl.pallas_call(
        functools.partial(_merge_kernel, n_groups=n_groups),
        grid=(t // tm,),
        in_specs=in_specs,
        out_specs=row(D_MODEL),
        out_shape=jax.ShapeDtypeStruct((t, D_MODEL), F32),
        compiler_params=_cparams("parallel"),
        name="merge",
    )(*ins)


def _prep_weights(norm_g, w_in, conv_w, conv_b, dt_bias, a_log, d_skip, ssm_norm,
                  w_branch_a, w_branch_b, w_out, final_norm):
    sizes = (D_ATTN, D_ATTN, D_ATTN, D_ATTN, D_SSM, D_CONV, N_SSM_HEADS, D_MODEL, D_MODEL)
    pts = np.cumsum((0,) + sizes)
    seg = lambda i: w_in[:, pts[i]:pts[i + 1]]
    pad_heads = lambda v: jnp.pad(v.astype(F32), (0, LANES - N_SSM_HEADS)).reshape(1, LANES)
    w_a = w_in[:, 0:pts[4]].astype(BF16)
    w_b = jnp.concatenate([seg(4), seg(7), seg(8)], axis=1).astype(BF16)
    w_c = jnp.concatenate([seg(5), jnp.pad(seg(6), ((0, 0), (0, LANES - N_SSM_HEADS)))], axis=1).astype(BF16)
    tri = jnp.asarray(np.tril(np.ones((CHUNK, CHUNK), np.float32)))
    expand = jnp.asarray(
        (np.arange(LANES)[:, None] == np.arange(D_ATTN)[None, :] // HEAD_DIM_A).astype(np.float32)).astype(BF16)
    ssd_w = (conv_w.astype(F32), conv_b.astype(F32).reshape(1, D_CONV), pad_heads(dt_bias),
             pad_heads(-jnp.exp(a_log.astype(F32))),
             jnp.repeat(d_skip.astype(F32), SSM_HEAD_DIM).reshape(1, D_SSM),
             ssm_norm.astype(F32).reshape(1, D_SSM), tri)
    return dict(norm_g=norm_g.astype(F32).reshape(1, D_MODEL), w_a=w_a, w_b=w_b, w_c=w_c, ssd_w=ssd_w,
                expand=expand, wa=w_branch_a.astype(BF16), wb=w_branch_b.astype(BF16),
                wo=w_out.astype(BF16), fnorm=final_norm.astype(F32).reshape(1, D_MODEL))


def _layer(x, cache_k, cache_v, conv_past, ssm_past, wts, rel_bias):
    bsz, seq, _ = x.shape
    t = bsz * seq
    x2 = x.reshape(t, D_MODEL)
    h, qb, k, v, kb, vb, g_attn = _proj_a(x2, wts["norm_g"], wts["w_a"])
    z, gate_a, gate_b = _proj(h, wts["w_b"], (D_SSM, D_MODEL, D_MODEL), "proj_b")
    xs, bm, cm, dt = _proj(h, wts["w_c"], (D_SSM, D_BC, D_BC, LANES), "proj_c")

    if cache_k is None:
        os_, lses = [], []
        for d in DILATIONS:
            o, lse = _attn_group(qb, kb, vb, _window_bias(rel_bias, d), bsz, seq, d)
            os_.append(o)
            lses.append(lse)
        past8 = jnp.zeros((bsz, SUBLANES, D_CONV), F32)
    else:
        buf = cache_k.shape[1]
        o = _attn_sample(qb, k, v, cache_k.reshape(bsz, buf, D_ATTN), cache_v.reshape(bsz, buf, D_ATTN),
                         _sample_bias(rel_bias, seq, buf), bsz, seq)
        os_, lses = [o], []
        past8 = jnp.pad(conv_past.astype(F32), ((0, 0), (SUBLANES - (CONV_WIDTH - 1), 0), (0, 0)))

    h0 = None if ssm_past is None else ssm_past.astype(F32).reshape(bsz, D_SSM, D_STATE)
    ys, ssm_new = _ssd(xs, bm, cm, dt, z, past8, h0, wts["ssd_w"], bsz, seq)
    y = _merge(os_, lses, g_attn, ys, gate_a, gate_b, x2, wts["expand"], wts["wa"], wts["wb"],
               wts["wo"], wts["fnorm"])

    tail = lambda a: a.reshape(bsz, seq, -1)[:, seq - (CONV_WIDTH - 1):]
    conv_new = jnp.concatenate([tail(xs), tail(bm), tail(cm)], axis=-1)
    if seq < CONV_WIDTH - 1:
        conv_new = jnp.concatenate([conv_past, conv_new], axis=1)[:, -(CONV_WIDTH - 1):]
    return (y.reshape(bsz, seq, D_MODEL), k.reshape(bsz, seq, N_HEADS_A, HEAD_DIM_A),
            v.reshape(bsz, seq, N_HEADS_A, HEAD_DIM_A), conv_new,
            ssm_new.reshape(bsz, N_SSM_HEADS, SSM_HEAD_DIM, D_STATE))


def kernel(x_prompt, x_sample, cache_k, cache_v, state_conv, state_ssm, norm_g, w_in, conv_w, conv_b,
           dt_bias, a_log, d_skip, ssm_norm, w_branch_a, w_branch_b, w_out, rel_bias, final_norm):
    assert w_in.shape[0] == 1, "single layer"
    wts = _prep_weights(norm_g[0], w_in[0], conv_w[0], conv_b[0], dt_bias[0], a_log[0], d_skip[0],
                        ssm_norm[0], w_branch_a[0], w_branch_b[0], w_out[0], final_norm)
    keep = min(WINDOW_MAX, x_prompt.shape[1])
    yp, kp, vp, cp, sp = _layer(x_prompt, None, None, None, None, wts, rel_bias)
    ys, ks, vs, cs, ss = _layer(x_sample, cache_k[0], cache_v[0], state_conv[0], state_ssm[0], wts, rel_bias)
    return (yp, ys, kp[:, -keep:][None], vp[:, -keep:][None], cp[None], sp[None],
            ks[None], vs[None], cs[None], ss[None])
```

```python
import functools
import math

import numpy as np
import jax
import jax.numpy as jnp
from jax import lax
from jax.experimental import pallas as pl
from jax.experimental.pallas import tpu as pltpu

F32 = jnp.float32
BF16 = jnp.bfloat16

D_MODEL = 1024
N_HEADS_A = 12
HEAD_DIM_A = 64
D_ATTN = N_HEADS_A * HEAD_DIM_A
DILATIONS = (1, 4, 16)
WINDOW_KEYS = 128
WINDOW_MAX = 2048
N_BUCKETS = 32
MAX_DISTANCE = WINDOW_MAX
D_SSM = 2048
SSM_HEAD_DIM = 64
N_SSM_HEADS = D_SSM // SSM_HEAD_DIM
SSM_GROUPS = 8
HEADS_PER_GROUP = N_SSM_HEADS // SSM_GROUPS
GROUP_WIDTH = D_SSM // SSM_GROUPS
D_STATE = 128
CONV_WIDTH = 4
D_BC = SSM_GROUPS * D_STATE
D_CONV = D_SSM + 2 * D_BC
CHUNK = 128
LANES = 128
SUBLANES = 8
NEG = -1e30
BLOCKS_PER_STEP = 4
VMEM_LIMIT = 48 * 1024 * 1024


def _cparams(*sem):
    return pltpu.CompilerParams(dimension_semantics=sem, vmem_limit_bytes=VMEM_LIMIT)


def _dot(a, b):
    return jnp.dot(a, b, preferred_element_type=F32)


def _dot_nt(a, b):
    return lax.dot_general(a, b, (((1,), (1,)), ((), ())), preferred_element_type=F32)


def _silu(x):
    return x * (1.0 / (1.0 + jnp.exp(-x)))


def _sigmoid(x):
    return 1.0 / (1.0 + jnp.exp(-x))


def _div_pow2(x, n):
    assert n & (n - 1) == 0
    return jnp.right_shift(x, int(math.log2(n)))


def _proj_a_kernel(x_ref, g_ref, w_ref, h_ref, q_ref, k_ref, v_ref, ga_ref, *t_refs):
    xf = x_ref[...]
    ms = jnp.mean(xf * xf, axis=-1, keepdims=True)
    h = (xf * lax.rsqrt(ms + 1e-6) * g_ref[...]).astype(BF16)
    h_ref[...] = h
    q_ref[...] = _dot(h, w_ref[:, 0:D_ATTN]) * (1.0 / math.sqrt(HEAD_DIM_A))
    k = _dot(h, w_ref[:, D_ATTN:2 * D_ATTN])
    k_ref[...] = k
    v = _dot(h, w_ref[:, 2 * D_ATTN:3 * D_ATTN])
    v_ref[...] = v
    ga_ref[...] = _dot(h, w_ref[:, 3 * D_ATTN:4 * D_ATTN])
    if t_refs:
        kt_ref, vt_ref = t_refs
        kt_ref[0] = k.T
        vt_ref[0] = v.T


def _proj_kernel(h_ref, w_ref, *out_refs, widths):
    h = h_ref[...]
    c0 = 0
    for o_ref, wd in zip(out_refs, widths):
        o_ref[...] = _dot(h, w_ref[:, c0:c0 + wd])
        c0 += wd


def _row_tile(t):
    return 512 if t % 512 == 0 else t


def _proj_a(x, norm_g, w_a, bsz, seq):
    t = x.shape[0]
    tm = _row_tile(t)
    row = lambda n: pl.BlockSpec((tm, n), lambda i: (i, 0))
    full = lambda a: pl.BlockSpec(a.shape, lambda i: (0, 0))
    outs = [jax.ShapeDtypeStruct((t, D_MODEL), BF16)] + [jax.ShapeDtypeStruct((t, D_ATTN), F32)] * 4
    out_specs = [row(s.shape[1]) for s in outs]
    if seq % tm == 0:
        per_seq = seq // tm
        outs += [jax.ShapeDtypeStruct((bsz, D_ATTN, seq), F32)] * 2
        out_specs += [pl.BlockSpec((1, D_ATTN, tm), lambda i: (i // per_seq, 0, i % per_seq))] * 2
    return pl.pallas_call(
        _proj_a_kernel,
        grid=(t // tm,),
        in_specs=[row(D_MODEL), full(norm_g), full(w_a)],
        out_specs=out_specs,
        out_shape=outs,
        compiler_params=_cparams("parallel"),
        name="proj_a",
    )(x, norm_g, w_a)


def _proj(h, w, widths, name):
    t = h.shape[0]
    tm = _row_tile(t)
    row = lambda n: pl.BlockSpec((tm, n), lambda i: (i, 0))
    outs = [jax.ShapeDtypeStruct((t, wd), F32) for wd in widths]
    return pl.pallas_call(
        functools.partial(_proj_kernel, widths=widths),
        grid=(t // tm,),
        in_specs=[row(D_MODEL), pl.BlockSpec(w.shape, lambda i: (0, 0))],
        out_specs=[row(wd) for wd in widths],
        out_shape=outs,
        compiler_params=_cparams("parallel"),
        name=name,
    )(h, w)


def _t5_bucket(dist):
    max_exact = N_BUCKETS // 2
    d = np.maximum(dist, 1).astype(np.float32)
    large = max_exact + (np.log(d / max_exact) / math.log(MAX_DISTANCE / max_exact)
                         * (N_BUCKETS - max_exact)).astype(np.int32)
    large = np.minimum(large, N_BUCKETS - 1)
    return np.where(dist < max_exact, dist, large).astype(np.int32)


def _bias_by_distance(rel_bias, dist, valid, extra=None):
    onehot = (_t5_bucket(np.maximum(dist, 0))[:, None] == np.arange(N_BUCKETS)[None, :]) & valid[:, None]
    b = jnp.dot(jnp.asarray(onehot.astype(np.float32)), rel_bias.astype(F32), precision=lax.Precision.HIGHEST)
    if extra is not None:
        b = b + jnp.asarray(extra.astype(np.float32))[:, None]
    return jnp.where(jnp.asarray(valid)[:, None], b, NEG)


def _window_bias(rel_bias):
    u = np.arange(2 * CHUNK)
    out = []
    for d in DILATIONS:
        g = _bias_by_distance(rel_bias, (WINDOW_KEYS - u) * d, u <= WINDOW_KEYS).T
        reps = jnp.tile(g, (1, CHUNK * 2))[:, :CHUNK * (4 * CHUNK - 1)]
        toep = reps.reshape(N_HEADS_A, CHUNK, 4 * CHUNK - 1)[:, :, :2 * CHUNK]
        out.append(toep.reshape(N_HEADS_A // 2, 2 * CHUNK, 2 * CHUNK))
    return jnp.stack(out)


def _sample_bias(rel_bias, lq, buf):
    ncol = buf + CHUNK
    dist = np.arange(-(CHUNK + lq), buf + lq)
    mult = np.zeros(dist.shape, np.int32)
    for d in DILATIONS:
        mult += ((dist >= 0) & (dist % d == 0) & (dist <= WINDOW_KEYS * d)).astype(np.int32)
    f = _bias_by_distance(rel_bias, dist, mult > 0, np.log(np.maximum(mult, 1))).T
    fr = f[:, ::-1]
    rows = [fr[:, lq - 1 - i:lq - 1 - i + ncol] for i in range(lq)]
    return jnp.stack(rows, axis=1).reshape(N_HEADS_A * lq, ncol)


def _attn_prompt_kernel(q_ref, k_ref, v_ref, bias_ref, o_ref, m_ref, l_ref, *, seq):
    lane = lax.broadcasted_iota(jnp.int32, (CHUNK, LANES), 1)
    first_half = lane < HEAD_DIM_A
    col = lax.broadcasted_iota(jnp.int32, (2 * CHUNK, 2 * CHUNK), 1)
    last_group = len(DILATIONS) - 1

    def rows(ref, start, d):
        if d == 1:
            return ref[0, pl.ds(start, CHUNK), :]
        return ref[0, pl.ds(start, CHUNK, stride=d), :]

    def put(ref, start, d, val):
        if d == 1:
            ref[0, pl.ds(start, CHUNK), :] = val
        else:
            ref[0, pl.ds(start, CHUNK, stride=d), :] = val

    def pair_tile(x):
        return jnp.where(first_half, x[:CHUNK], x[CHUNK:])

    def scores(gi, d, start, prev_start, pen):
        qp = rows(q_ref, start, d)
        qs = jnp.concatenate([jnp.where(first_half, qp, 0.0),
                              jnp.where(first_half, 0.0, qp)], axis=0).astype(BF16)
        kw = rows(k_ref, start, d).astype(BF16)
        vw = rows(v_ref, start, d).astype(BF16)
        if prev_start is None:
            s = _dot_nt(qs, kw) + bias_ref[gi, 0, :, CHUNK:]
        else:
            kw = jnp.concatenate([rows(k_ref, prev_start, d).astype(BF16), kw], axis=0)
            vw = jnp.concatenate([rows(v_ref, prev_start, d).astype(BF16), vw], axis=0)
            s = _dot_nt(qs, kw) + bias_ref[gi, 0] + jnp.where(col < CHUNK, pen, 0.0)
        m = jnp.max(s, axis=1, keepdims=True)
        e = jnp.exp(s - m)
        l = pair_tile(jnp.sum(e, axis=1, keepdims=True))
        pv = pair_tile(_dot(e.astype(BF16), vw))
        return pv, pair_tile(m), l

    def accumulate(gi, d, start, pv, m, l):
        if gi > 0:
            m_old = rows(m_ref, start, d)
            m_new = jnp.maximum(m_old, m)
            a = jnp.exp(m_old - m_new)
            b = jnp.exp(m - m_new)
            pv = a * rows(o_ref, start, d) + b * pv
            l = a * rows(l_ref, start, d) + b * l
            m = m_new
        if gi == last_group:
            put(o_ref, start, d, pv * (1.0 / l))
        else:
            put(o_ref, start, d, pv)
            put(m_ref, start, d, m)
            put(l_ref, start, d, l)

    for gi, d in enumerate(DILATIONS):
        lc = seq // d
        nblk = lc // CHUNK

        def body(it, carry, gi=gi, d=d, nblk=nblk):
            starts, parts = [], []
            for u in range(BLOCKS_PER_STEP):
                blk = it * BLOCKS_PER_STEP + u
                if nblk == 1:
                    start, prev, pen = blk, None, None
                else:
                    r = blk // nblk
                    i = blk % nblk
                    start = r + d * CHUNK * i
                    prev = r + d * CHUNK * jnp.maximum(i - 1, 0)
                    pen = jnp.where(i == 0, NEG, 0.0)
                    if d == 1:
                        start = pl.multiple_of(start, CHUNK)
                        prev = pl.multiple_of(prev, CHUNK)
                starts.append(start)
                parts.append(scores(gi, d, start, prev, pen))
            for start, part in zip(starts, parts):
                accumulate(gi, d, start, *part)
            return carry

        lax.fori_loop(0, d * nblk // BLOCKS_PER_STEP, body, 0)


def _attn_prompt(q, k, v, bias, bsz, seq):
    assert all(seq % (d * CHUNK) == 0 for d in DILATIONS)
    pairs = N_HEADS_A // 2
    view = lambda a: a.reshape(bsz, seq, D_ATTN)
    blk = pl.BlockSpec((1, seq, LANES), lambda b, p: (b, 0, p))
    o = pl.pallas_call(
        functools.partial(_attn_prompt_kernel, seq=seq),
        grid=(bsz, pairs),
        in_specs=[blk, blk, blk,
                  pl.BlockSpec((len(DILATIONS), 1, 2 * CHUNK, 2 * CHUNK), lambda b, p: (0, p, 0, 0))],
        out_specs=blk,
        out_shape=jax.ShapeDtypeStruct((bsz, seq, D_ATTN), F32),
        scratch_shapes=[pltpu.VMEM((1, seq, LANES), F32), pltpu.VMEM((1, seq, LANES), F32)],
        compiler_params=_cparams("parallel", "parallel"),
        name="attn_prompt",
    )(view(q), view(k), view(v), bias)
    return o.reshape(bsz * seq, D_ATTN)


def _attn_sample_kernel(q_ref, kn_ref, vn_ref, ck_ref, cv_ref, bias_ref, o_ref, *, lq, buf):
    nrow = N_HEADS_A * lq
    pad = jnp.zeros((CHUNK - lq, D_ATTN), F32)
    row_head = _div_pow2(lax.broadcasted_iota(jnp.int32, (nrow, D_ATTN), 0), lq)
    col_head = _div_pow2(lax.broadcasted_iota(jnp.int32, (nrow, D_ATTN), 1), HEAD_DIM_A)
    own = row_head == col_head
    qs = jnp.where(own, jnp.concatenate([q_ref[0]] * N_HEADS_A, axis=0), 0.0).astype(BF16)
    kn = jnp.concatenate([kn_ref[0], pad], axis=0).astype(BF16)
    vn = jnp.concatenate([vn_ref[0], pad], axis=0).astype(BF16)
    s_c = _dot(qs, ck_ref[0].astype(BF16)) + bias_ref[:, 0:buf]
    s_n = _dot_nt(qs, kn) + bias_ref[:, buf:buf + CHUNK]
    m = jnp.maximum(jnp.max(s_c, axis=1, keepdims=True), jnp.max(s_n, axis=1, keepdims=True))
    e_c = jnp.exp(s_c - m)
    e_n = jnp.exp(s_n - m)
    l = jnp.sum(e_c, axis=1, keepdims=True) + jnp.sum(e_n, axis=1, keepdims=True)
    pv = _dot_nt(e_c.astype(BF16), cv_ref[0].astype(BF16)) + _dot(e_n.astype(BF16), vn)
    pv = jnp.where(own, pv * (1.0 / l), 0.0)
    o = pv[0:lq]
    for h in range(1, N_HEADS_A):
        o = o + pv[h * lq:(h + 1) * lq]
    o_ref[0] = o


def _attn_sample(q, k_new, v_new, cache_kt, cache_vt, bias, bsz, lq):
    buf = cache_kt.shape[2]
    assert lq & (lq - 1) == 0 and lq <= CHUNK
    tok = pl.BlockSpec((1, lq, D_ATTN), lambda b: (b, 0, 0))
    cache = pl.BlockSpec((1, D_ATTN, buf), lambda b: (b, 0, 0))
    per_seq = lambda a: a.reshape(bsz, lq, D_ATTN)
    o = pl.pallas_call(
        functools.partial(_attn_sample_kernel, lq=lq, buf=buf),
        grid=(bsz,),
        in_specs=[tok, tok, tok, cache, cache, pl.BlockSpec(bias.shape, lambda b: (0, 0))],
        out_specs=tok,
        out_shape=jax.ShapeDtypeStruct((bsz, lq, D_ATTN), F32),
        compiler_params=_cparams("parallel"),
        name="attn_sample",
    )(per_seq(q), per_seq(k_new), per_seq(v_new), cache_kt, cache_vt, bias)
    return o.reshape(bsz * lq, D_ATTN)


def _ssd_kernel(*refs, rows, has_state):
    (xs_ref, bm_ref, cm_ref, dt_ref, z_ref, past_ref) = refs[:6]
    refs = refs[6:]
    if has_state:
        h0_ref = refs[0]
        refs = refs[1:]
    (cw_ref, cb_ref, dtb_ref, a_ref, dsk_ref, nrm_ref, tri_ref,
     y_ref, hout_ref, xpad_ref, xc_ref) = refs
    c = pl.program_id(1)

    @pl.when(c == 0)
    def _():
        xpad_ref[0:SUBLANES, :] = past_ref[0]
        if has_state:
            hout_ref[0] = h0_ref[0]
        else:
            hout_ref[0] = jnp.zeros((D_SSM, D_STATE), F32)
        if rows < CHUNK:
            xpad_ref[SUBLANES + rows:SUBLANES + CHUNK, :] = jnp.zeros((CHUNK - rows, D_CONV), F32)

    @pl.when(c > 0)
    def _():
        xpad_ref[0:SUBLANES, :] = xpad_ref[CHUNK:CHUNK + SUBLANES, :]

    xpad_ref[SUBLANES:SUBLANES + rows, 0:D_SSM] = xs_ref[0]
    xpad_ref[SUBLANES:SUBLANES + rows, D_SSM:D_SSM + D_BC] = bm_ref[0]
    xpad_ref[SUBLANES:SUBLANES + rows, D_SSM + D_BC:D_CONV] = cm_ref[0]

    slab = 512
    for j in range(D_CONV // slab):
        cs = slice(j * slab, (j + 1) * slab)
        acc = cb_ref[:, cs]
        for tap in range(CONV_WIDTH):
            off = SUBLANES - (CONV_WIDTH - 1) + tap
            acc = acc + xpad_ref[off:off + CHUNK, cs] * cw_ref[tap:tap + 1, cs]
        xc_ref[:, cs] = _silu(acc)

    dt_raw = dt_ref[0]
    if rows < CHUNK:
        dt_raw = jnp.concatenate([dt_raw, jnp.zeros((CHUNK - rows, LANES), F32)], axis=0)
    pre = dt_raw + dtb_ref[...]
    dt = jnp.maximum(pre, 0.0) + jnp.log(1.0 + jnp.exp(-jnp.abs(pre)))
    if rows < CHUNK:
        rid = lax.broadcasted_iota(jnp.int32, (CHUNK, LANES), 0)
        dt = jnp.where(rid < rows, dt, 0.0)
    da = dt * a_ref[...]
    acum = jnp.dot(tri_ref[...], da, preferred_element_type=F32, precision=lax.Precision.HIGHEST)
    acum_t = acum.T
    last = acum[CHUNK - 1:CHUNK, :]
    e_acum = jnp.exp(acum)
    w_coef = dt * jnp.exp(last - acum)

    ri = lax.broadcasted_iota(jnp.int32, (CHUNK, CHUNK), 0)
    ci = lax.broadcasted_iota(jnp.int32, (CHUNK, CHUNK), 1)
    causal = ri >= ci
    glane = _div_pow2(lax.broadcasted_iota(jnp.int32, (CHUNK, GROUP_WIDTH), 1), SSM_HEAD_DIM)
    grow = _div_pow2(lax.broadcasted_iota(jnp.int32, (GROUP_WIDTH, D_STATE), 0), SSM_HEAD_DIM)

    def per_head_lanes(mat, g):
        out = jnp.broadcast_to(mat[:, HEADS_PER_GROUP * g:HEADS_PER_GROUP * g + 1], (CHUNK, GROUP_WIDTH))
        for k in range(1, HEADS_PER_GROUP):
            h = HEADS_PER_GROUP * g + k
            out = jnp.where(glane == k, jnp.broadcast_to(mat[:, h:h + 1], (CHUNK, GROUP_WIDTH)), out)
        return out

    for g in range(SSM_GROUPS):
        gs = slice(g * GROUP_WIDTH, (g + 1) * GROUP_WIDTH)
        bg = xc_ref[:, D_SSM + g * D_STATE:D_SSM + (g + 1) * D_STATE].astype(BF16)
        cg = xc_ref[:, D_SSM + D_BC + g * D_STATE:D_SSM + D_BC + (g + 1) * D_STATE].astype(BF16)
        xg = xc_ref[:, gs]
        cb = _dot_nt(cg, bg)
        xdt = xg * per_head_lanes(dt, g)
        m_parts, x_parts = [], []
        for k in range(HEADS_PER_GROUP):
            h = HEADS_PER_GROUP * g + k
            seg = acum[:, h:h + 1] - acum_t[h:h + 1, :]
            decay = jnp.exp(jnp.where(causal, seg, NEG))
            m_parts.append((cb * decay).astype(BF16))
            x_parts.append(jnp.where(glane == k, xdt, 0.0).astype(BF16))
        y = _dot(jnp.concatenate(m_parts, axis=1), jnp.concatenate(x_parts, axis=0))
        hg = hout_ref[0, gs, :]
        y = y + _dot_nt(cg, hg.astype(BF16)) * per_head_lanes(e_acum, g)
        y = y + dsk_ref[:, gs] * xg
        wx_t = (xg * per_head_lanes(w_coef, g)).T.astype(BF16)
        cdec = jnp.broadcast_to(last[:, HEADS_PER_GROUP * g:HEADS_PER_GROUP * g + 1], (GROUP_WIDTH, D_STATE))
        for k in range(1, HEADS_PER_GROUP):
            h = HEADS_PER_GROUP * g + k
            cdec = jnp.where(grow == k, jnp.broadcast_to(last[:, h:h + 1], (GROUP_WIDTH, D_STATE)), cdec)
        hout_ref[0, gs, :] = jnp.exp(cdec) * hg + _dot(wx_t, bg)
        yz = y[:rows] * _silu(z_ref[0, :, gs])
        ms = jnp.mean(yz * yz, axis=1, keepdims=True)
        y_ref[0, :, gs] = (yz * lax.rsqrt(ms + 1e-5) * nrm_ref[:, gs]).astype(y_ref.dtype)


def _ssd(xs, bm, cm, dt, z, past8, h0, wts, bsz, seq):
    rows = CHUNK if seq % CHUNK == 0 else seq
    nc = seq // rows
    tok = lambda n: pl.BlockSpec((1, rows, n), lambda b, c: (b * nc + c, 0, 0))
    full2 = lambda a: pl.BlockSpec(a.shape, lambda b, c: (0, 0))
    state = pl.BlockSpec((1, D_SSM, D_STATE), lambda b, c: (b, 0, 0))
    has_state = h0 is not None
    chunks = lambda a: a.reshape(bsz * nc, rows, a.shape[-1])
    ins = ([chunks(a) for a in (xs, bm, cm, dt, z)] + [past8] + ([h0] if has_state else []) + list(wts))
    in_specs = ([tok(D_SSM), tok(D_BC), tok(D_BC), tok(LANES), tok(D_SSM),
                 pl.BlockSpec((1, SUBLANES, D_CONV), lambda b, c: (b, 0, 0))]
                + ([state] if has_state else []) + [full2(w) for w in wts])
    y, h_new = pl.pallas_call(
        functools.partial(_ssd_kernel, rows=rows, has_state=has_state),
        grid=(bsz, nc),
        in_specs=in_specs,
        out_specs=[tok(D_SSM), state],
        out_shape=[jax.ShapeDtypeStruct((bsz * nc, rows, D_SSM), BF16),
                   jax.ShapeDtypeStruct((bsz, D_SSM, D_STATE), F32)],
        scratch_shapes=[pltpu.VMEM((SUBLANES + CHUNK, D_CONV), F32),
                        pltpu.VMEM((CHUNK, D_CONV), F32)],
        compiler_params=_cparams("parallel", "arbitrary"),
        name="ssd",
    )(*ins)
    return y.reshape(bsz * seq, D_SSM), h_new


def _merge_kernel(o_ref, g_ref, ys_ref, ga_ref, gb_ref, x_ref, wa_ref, wb_ref, wo_ref, fn_ref, y_ref):
    ya = (o_ref[...] * _silu(g_ref[...])).astype(BF16)
    merged = (_sigmoid(ga_ref[...]) * _dot(ya, wa_ref[...])
              + _sigmoid(gb_ref[...]) * _dot(ys_ref[...], wb_ref[...]))
    out = x_ref[...] + _dot(merged.astype(BF16), wo_ref[...])
    ms = jnp.mean(out * out, axis=-1, keepdims=True)
    y_ref[...] = out * lax.rsqrt(ms + 1e-6) * fn_ref[...]


def _merge(o, g_attn, ys, gate_a, gate_b, x, wa, wb, wo, fnorm):
    t = x.shape[0]
    tm = _row_tile(t)
    row = lambda n: pl.BlockSpec((tm, n), lambda i: (i, 0))
    full = lambda a: pl.BlockSpec(a.shape, lambda i: (0, 0))
    return pl.pallas_call(
        _merge_kernel,
        grid=(t // tm,),
        in_specs=[row(D_ATTN), row(D_ATTN), row(D_SSM), row(D_MODEL), row(D_MODEL), row(D_MODEL),
                  full(wa), full(wb), full(wo), full(fnorm)],
        out_specs=row(D_MODEL),
        out_shape=jax.ShapeDtypeStruct((t, D_MODEL), F32),
        compiler_params=_cparams("parallel"),
        name="merge",
    )(o, g_attn, ys, gate_a, gate_b, x, wa, wb, wo, fnorm)


def _prep_weights(norm_g, w_in, conv_w, conv_b, dt_bias, a_log, d_skip, ssm_norm,
                  w_branch_a, w_branch_b, w_out, final_norm):
    sizes = (D_ATTN, D_ATTN, D_ATTN, D_ATTN, D_SSM, D_CONV, N_SSM_HEADS, D_MODEL, D_MODEL)
    pts = np.cumsum((0,) + sizes)
    seg = lambda i: w_in[:, pts[i]:pts[i + 1]]
    pad_heads = lambda v: jnp.pad(v.astype(F32), (0, LANES - N_SSM_HEADS)).reshape(1, LANES)
    w_a = w_in[:, 0:pts[4]].astype(BF16)
    w_b = jnp.concatenate([seg(4), seg(7), seg(8)], axis=1).astype(BF16)
    w_c = jnp.concatenate([seg(5), jnp.pad(seg(6), ((0, 0), (0, LANES - N_SSM_HEADS)))], axis=1).astype(BF16)
    tri = jnp.asarray(np.tril(np.ones((CHUNK, CHUNK), np.float32)))
    ssd_w = (conv_w.astype(F32), conv_b.astype(F32).reshape(1, D_CONV), pad_heads(dt_bias),
             pad_heads(-jnp.exp(a_log.astype(F32))),
             jnp.repeat(d_skip.astype(F32), SSM_HEAD_DIM).reshape(1, D_SSM),
             ssm_norm.astype(F32).reshape(1, D_SSM), tri)
    return dict(norm_g=norm_g.astype(F32).reshape(1, D_MODEL), w_a=w_a, w_b=w_b, w_c=w_c, ssd_w=ssd_w,
                wa=w_branch_a.astype(BF16), wb=w_branch_b.astype(BF16),
                wo=w_out.astype(BF16), fnorm=final_norm.astype(F32).reshape(1, D_MODEL))


def _layer(x, cache_k, cache_v, conv_past, ssm_past, wts, rel_bias):
    bsz, seq, _ = x.shape
    t = bsz * seq
    x2 = x.reshape(t, D_MODEL)
    h, q, k, v, g_attn, *kv_t = _proj_a(x2, wts["norm_g"], wts["w_a"], bsz, seq)
    z, gate_a, gate_b = _proj(h, wts["w_b"], (D_SSM, D_MODEL, D_MODEL), "proj_b")
    xs, bm, cm, dt = _proj(h, wts["w_c"], (D_SSM, D_BC, D_BC, LANES), "proj_c")

    if cache_k is None:
        o = _attn_prompt(q, k, v, _window_bias(rel_bias), bsz, seq)
        past8 = jnp.zeros((bsz, SUBLANES, D_CONV), F32)
    else:
        buf = cache_k.shape[1]
        feature_major = lambda a: jnp.transpose(a, (0, 2, 3, 1)).reshape(bsz, D_ATTN, buf)
        o = _attn_sample(q, k, v, feature_major(cache_k), feature_major(cache_v),
                         _sample_bias(rel_bias, seq, buf), bsz, seq)
        past8 = jnp.pad(conv_past.astype(F32), ((0, 0), (SUBLANES - (CONV_WIDTH - 1), 0), (0, 0)))

    h0 = None if ssm_past is None else ssm_past.astype(F32).reshape(bsz, D_SSM, D_STATE)
    ys, ssm_new = _ssd(xs, bm, cm, dt, z, past8, h0, wts["ssd_w"], bsz, seq)
    y = _merge(o, g_attn, ys, gate_a, gate_b, x2, wts["wa"], wts["wb"], wts["wo"], wts["fnorm"])

    tail = lambda a: a.reshape(bsz, seq, -1)[:, seq - (CONV_WIDTH - 1):]
    conv_new = jnp.concatenate([tail(xs), tail(bm), tail(cm)], axis=-1)
    if seq < CONV_WIDTH - 1:
        conv_new = jnp.concatenate([conv_past, conv_new], axis=1)[:, -(CONV_WIDTH - 1):]
    if kv_t:
        per_head = lambda a: jnp.transpose(a.reshape(bsz, N_HEADS_A, HEAD_DIM_A, seq), (0, 3, 1, 2))
        k_out, v_out = per_head(kv_t[0]), per_head(kv_t[1])
    else:
        k_out = k.reshape(bsz, seq, N_HEADS_A, HEAD_DIM_A)
        v_out = v.reshape(bsz, seq, N_HEADS_A, HEAD_DIM_A)
    return (y.reshape(bsz, seq, D_MODEL), k_out, v_out, conv_new,
            ssm_new.reshape(bsz, N_SSM_HEADS, SSM_HEAD_DIM, D_STATE))


def kernel(x_prompt, x_sample, cache_k, cache_v, state_conv, state_ssm, norm_g, w_in, conv_w, conv_b,
           dt_bias, a_log, d_skip, ssm_norm, w_branch_a, w_branch_b, w_out, rel_bias, final_norm):
    assert w_in.shape[0] == 1, "single layer"
    wts = _prep_weights(norm_g[0], w_in[0], conv_w[0], conv_b[0], dt_bias[0], a_log[0], d_skip[0],
                        ssm_norm[0], w_branch_a[0], w_branch_b[0], w_out[0], final_norm)
    keep = min(WINDOW_MAX, x_prompt.shape[1])
    yp, kp, vp, cp, sp = _layer(x_prompt, None, None, None, None, wts, rel_bias)
    ys, ks, vs, cs, ss = _layer(x_sample, cache_k[0], cache_v[0], state_conv[0], state_ssm[0], wts, rel_bias)
    return (yp, ys, kp[:, -keep:][None], vp[:, -keep:][None], cp[None], sp[None],
            ks[None], vs[None], cs[None], ss[None])
```

```python
import functools
import math

import numpy as np
import jax
import jax.numpy as jnp
from jax import lax
from jax.experimental import pallas as pl
from jax.experimental.pallas import tpu as pltpu

F32 = jnp.float32
BF16 = jnp.bfloat16

D_MODEL = 1024
N_HEADS_A = 12
HEAD_DIM_A = 64
D_ATTN = N_HEADS_A * HEAD_DIM_A
DILATIONS = (1, 4, 16)
WINDOW_KEYS = 128
WINDOW_MAX = 2048
N_BUCKETS = 32
MAX_DISTANCE = WINDOW_MAX
D_SSM = 2048
SSM_HEAD_DIM = 64
N_SSM_HEADS = D_SSM // SSM_HEAD_DIM
SSM_GROUPS = 8
HEADS_PER_GROUP = N_SSM_HEADS // SSM_GROUPS
GROUP_WIDTH = D_SSM // SSM_GROUPS
D_STATE = 128
CONV_WIDTH = 4
D_BC = SSM_GROUPS * D_STATE
D_CONV = D_SSM + 2 * D_BC
CHUNK = 128
LANES = 128
SUBLANES = 8
NEG = -1e30
BLOCKS_PER_STEP = 4
LOG2E = math.log2(math.e)
VMEM_LIMIT = 48 * 1024 * 1024


def _cparams(*sem):
    return pltpu.CompilerParams(dimension_semantics=sem, vmem_limit_bytes=VMEM_LIMIT)


def _dot(a, b):
    return jnp.dot(a, b, preferred_element_type=F32)


def _dot_nt(a, b):
    return lax.dot_general(a, b, (((1,), (1,)), ((), ())), preferred_element_type=F32)


def _silu(x):
    return x * (1.0 / (1.0 + jnp.exp(-x)))


def _sigmoid(x):
    return 1.0 / (1.0 + jnp.exp(-x))


def _div_pow2(x, n):
    assert n & (n - 1) == 0
    return jnp.right_shift(x, int(math.log2(n)))


def _proj_a_kernel(x_ref, g_ref, w_ref, h_ref, q_ref, k_ref, v_ref, ga_ref, *t_refs):
    xf = x_ref[...]
    ms = jnp.mean(xf * xf, axis=-1, keepdims=True)
    h = (xf * lax.rsqrt(ms + 1e-6) * g_ref[...]).astype(BF16)
    h_ref[...] = h
    q_ref[...] = _dot(h, w_ref[:, 0:D_ATTN]) * (LOG2E / math.sqrt(HEAD_DIM_A))
    k = _dot(h, w_ref[:, D_ATTN:2 * D_ATTN])
    k_ref[...] = k
    v = _dot(h, w_ref[:, 2 * D_ATTN:3 * D_ATTN])
    v_ref[...] = v
    ga_ref[...] = _dot(h, w_ref[:, 3 * D_ATTN:4 * D_ATTN])
    if t_refs:
        kt_ref, vt_ref = t_refs
        kt_ref[0] = k.T
        vt_ref[0] = v.T


def _proj_kernel(h_ref, w_ref, *out_refs, widths):
    h = h_ref[...]
    c0 = 0
    for o_ref, wd in zip(out_refs, widths):
        o_ref[...] = _dot(h, w_ref[:, c0:c0 + wd])
        c0 += wd


def _row_tile(t):
    return 512 if t % 512 == 0 else t


def _proj_a(x, norm_g, w_a, bsz, seq):
    t = x.shape[0]
    tm = _row_tile(t)
    row = lambda n: pl.BlockSpec((tm, n), lambda i: (i, 0))
    full = lambda a: pl.BlockSpec(a.shape, lambda i: (0, 0))
    outs = [jax.ShapeDtypeStruct((t, D_MODEL), BF16)] + [jax.ShapeDtypeStruct((t, D_ATTN), F32)] * 4
    out_specs = [row(s.shape[1]) for s in outs]
    if seq % tm == 0:
        per_seq = seq // tm
        outs += [jax.ShapeDtypeStruct((bsz, D_ATTN, seq), F32)] * 2
        out_specs += [pl.BlockSpec((1, D_ATTN, tm), lambda i: (i // per_seq, 0, i % per_seq))] * 2
    return pl.pallas_call(
        _proj_a_kernel,
        grid=(t // tm,),
        in_specs=[row(D_MODEL), full(norm_g), full(w_a)],
        out_specs=out_specs,
        out_shape=outs,
        compiler_params=_cparams("parallel"),
        name="proj_a",
    )(x, norm_g, w_a)


def _proj(h, w, widths, name):
    t = h.shape[0]
    tm = _row_tile(t)
    row = lambda n: pl.BlockSpec((tm, n), lambda i: (i, 0))
    outs = [jax.ShapeDtypeStruct((t, wd), F32) for wd in widths]
    return pl.pallas_call(
        functools.partial(_proj_kernel, widths=widths),
        grid=(t // tm,),
        in_specs=[row(D_MODEL), pl.BlockSpec(w.shape, lambda i: (0, 0))],
        out_specs=[row(wd) for wd in widths],
        out_shape=outs,
        compiler_params=_cparams("parallel"),
        name=name,
    )(h, w)


def _t5_bucket(dist):
    max_exact = N_BUCKETS // 2
    d = np.maximum(dist, 1).astype(np.float32)
    large = max_exact + (np.log(d / max_exact) / math.log(MAX_DISTANCE / max_exact)
                         * (N_BUCKETS - max_exact)).astype(np.int32)
    large = np.minimum(large, N_BUCKETS - 1)
    return np.where(dist < max_exact, dist, large).astype(np.int32)


def _bias_by_distance(rel_bias, dist, valid, extra=None):
    onehot = (_t5_bucket(np.maximum(dist, 0))[:, None] == np.arange(N_BUCKETS)[None, :]) & valid[:, None]
    b = jnp.dot(jnp.asarray(onehot.astype(np.float32)), rel_bias.astype(F32), precision=lax.Precision.HIGHEST)
    if extra is not None:
        b = b + jnp.asarray(extra.astype(np.float32))[:, None]
    return jnp.where(jnp.asarray(valid)[:, None], b * LOG2E, NEG)


def _window_bias(rel_bias):
    u = np.arange(2 * CHUNK)
    has_prev = jnp.asarray(np.arange(2 * CHUNK) >= CHUNK)
    out = []
    for d in DILATIONS:
        g = _bias_by_distance(rel_bias, (WINDOW_KEYS - u) * d, u <= WINDOW_KEYS).T
        reps = jnp.tile(g, (1, CHUNK * 2))[:, :CHUNK * (4 * CHUNK - 1)]
        toep = reps.reshape(N_HEADS_A, CHUNK, 4 * CHUNK - 1)[:, :, :2 * CHUNK]
        toep = toep.reshape(N_HEADS_A // 2, 2 * CHUNK, 2 * CHUNK)
        out.append(jnp.stack([toep, jnp.where(has_prev, toep, NEG)], axis=1))
    return jnp.stack(out)


def _sample_bias(rel_bias, lq, buf):
    ncol = buf + CHUNK
    dist = np.arange(-(CHUNK + lq), buf + lq)
    mult = np.zeros(dist.shape, np.int32)
    for d in DILATIONS:
        mult += ((dist >= 0) & (dist % d == 0) & (dist <= WINDOW_KEYS * d)).astype(np.int32)
    f = _bias_by_distance(rel_bias, dist, mult > 0, np.log(np.maximum(mult, 1))).T
    fr = f[:, ::-1]
    rows = [fr[:, lq - 1 - i:lq - 1 - i + ncol] for i in range(lq)]
    return jnp.stack(rows, axis=1).reshape(N_HEADS_A * lq, ncol)


def _attn_prompt_kernel(q_ref, k_ref, v_ref, bias_ref, o_ref, m_ref, l_ref, *, seq):
    lane = lax.broadcasted_iota(jnp.int32, (CHUNK, LANES), 1)
    first_half = lane < HEAD_DIM_A
    last_group = len(DILATIONS) - 1

    def rows(ref, start, d):
        if d == 1:
            return ref[0, pl.ds(start, CHUNK), :]
        return ref[0, pl.ds(start, CHUNK, stride=d), :]

    def put(ref, start, d, val):
        if d == 1:
            ref[0, pl.ds(start, CHUNK), :] = val
        else:
            ref[0, pl.ds(start, CHUNK, stride=d), :] = val

    def pair_tile(x):
        return jnp.where(first_half, x[:CHUNK], x[CHUNK:])

    def scores(gi, d, start, prev_start, first):
        qp = rows(q_ref, start, d)
        qs = jnp.concatenate([jnp.where(first_half, qp, 0.0),
                              jnp.where(first_half, 0.0, qp)], axis=0).astype(BF16)
        kw = rows(k_ref, start, d).astype(BF16)
        vw = rows(v_ref, start, d).astype(BF16)
        if prev_start is None:
            s = _dot_nt(qs, kw) + bias_ref[gi, 0, 0, :, CHUNK:]
        else:
            kw = jnp.concatenate([rows(k_ref, prev_start, d).astype(BF16), kw], axis=0)
            vw = jnp.concatenate([rows(v_ref, prev_start, d).astype(BF16), vw], axis=0)
            s = _dot_nt(qs, kw) + bias_ref[gi, 0, first]
        m = jnp.max(s, axis=1, keepdims=True)
        e = jnp.exp2(s - m)
        l = pair_tile(jnp.sum(e, axis=1, keepdims=True))
        pv = pair_tile(_dot(e.astype(BF16), vw))
        return pv, pair_tile(m), l

    def accumulate(gi, d, start, pv, m, l):
        if gi > 0:
            m_old = rows(m_ref, start, d)
            m_new = jnp.maximum(m_old, m)
            a = jnp.exp2(m_old - m_new)
            b = jnp.exp2(m - m_new)
            pv = a * rows(o_ref, start, d) + b * pv
            l = a * rows(l_ref, start, d) + b * l
            m = m_new
        if gi == last_group:
            put(o_ref, start, d, pv * (1.0 / l))
        else:
            put(o_ref, start, d, pv)
            put(m_ref, start, d, m)
            put(l_ref, start, d, l)

    for gi, d in enumerate(DILATIONS):
        lc = seq // d
        nblk = lc // CHUNK

        def body(it, carry, gi=gi, d=d, nblk=nblk):
            starts, parts = [], []
            for u in range(BLOCKS_PER_STEP):
                blk = it * BLOCKS_PER_STEP + u
                if nblk == 1:
                    start, prev, first = blk, None, None
                else:
                    r = blk // nblk
                    i = blk % nblk
                    start = r + d * CHUNK * i
                    prev = r + d * CHUNK * jnp.maximum(i - 1, 0)
                    first = jnp.where(i == 0, 1, 0)
                    if d == 1:
                        start = pl.multiple_of(start, CHUNK)
                        prev = pl.multiple_of(prev, CHUNK)
                starts.append(start)
                parts.append(scores(gi, d, start, prev, first))
            for start, part in zip(starts, parts):
                accumulate(gi, d, start, *part)
            return carry

        lax.fori_loop(0, d * nblk // BLOCKS_PER_STEP, body, 0)


def _attn_prompt(q, k, v, bias, bsz, seq):
    assert all(seq % (d * CHUNK) == 0 for d in DILATIONS)
    pairs = N_HEADS_A // 2
    view = lambda a: a.reshape(bsz, seq, D_ATTN)
    blk = pl.BlockSpec((1, seq, LANES), lambda b, p: (b, 0, p))
    o = pl.pallas_call(
        functools.partial(_attn_prompt_kernel, seq=seq),
        grid=(bsz, pairs),
        in_specs=[blk, blk, blk,
                  pl.BlockSpec((len(DILATIONS), 1, 2, 2 * CHUNK, 2 * CHUNK), lambda b, p: (0, p, 0, 0, 0))],
        out_specs=blk,
        out_shape=jax.ShapeDtypeStruct((bsz, seq, D_ATTN), F32),
        scratch_shapes=[pltpu.VMEM((1, seq, LANES), F32), pltpu.VMEM((1, seq, LANES), F32)],
        compiler_params=_cparams("parallel", "parallel"),
        name="attn_prompt",
    )(view(q), view(k), view(v), bias)
    return o.reshape(bsz * seq, D_ATTN)


def _attn_sample_kernel(q_ref, kn_ref, vn_ref, ck_ref, cv_ref, bias_ref, o_ref, *, lq, buf):
    nrow = N_HEADS_A * lq
    pad = jnp.zeros((CHUNK - lq, D_ATTN), F32)
    row_head = _div_pow2(lax.broadcasted_iota(jnp.int32, (nrow, D_ATTN), 0), lq)
    col_head = _div_pow2(lax.broadcasted_iota(jnp.int32, (nrow, D_ATTN), 1), HEAD_DIM_A)
    own = row_head == col_head
    qs = jnp.where(own, jnp.concatenate([q_ref[0]] * N_HEADS_A, axis=0), 0.0).astype(BF16)
    kn = jnp.concatenate([kn_ref[0], pad], axis=0).astype(BF16)
    vn = jnp.concatenate([vn_ref[0], pad], axis=0).astype(BF16)
    s_c = _dot(qs, ck_ref[0].astype(BF16)) + bias_ref[:, 0:buf]
    s_n = _dot_nt(qs, kn) + bias_ref[:, buf:buf + CHUNK]
    m = jnp.maximum(jnp.max(s_c, axis=1, keepdims=True), jnp.max(s_n, axis=1, keepdims=True))
    e_c = jnp.exp2(s_c - m)
    e_n = jnp.exp2(s_n - m)
    l = jnp.sum(e_c, axis=1, keepdims=True) + jnp.sum(e_n, axis=1, keepdims=True)
    pv = _dot_nt(e_c.astype(BF16), cv_ref[0].astype(BF16)) + _dot(e_n.astype(BF16), vn)
    pv = jnp.where(own, pv * (1.0 / l), 0.0)
    o = pv[0:lq]
    for h in range(1, N_HEADS_A):
        o = o + pv[h * lq:(h + 1) * lq]
    o_ref[0] = o


def _attn_sample(q, k_new, v_new, cache_kt, cache_vt, bias, bsz, lq):
    buf = cache_kt.shape[2]
    assert lq & (lq - 1) == 0 and lq <= CHUNK
    tok = pl.BlockSpec((1, lq, D_ATTN), lambda b: (b, 0, 0))
    cache = pl.BlockSpec((1, D_ATTN, buf), lambda b: (b, 0, 0))
    per_seq = lambda a: a.reshape(bsz, lq, D_ATTN)
    o = pl.pallas_call(
        functools.partial(_attn_sample_kernel, lq=lq, buf=buf),
        grid=(bsz,),
        in_specs=[tok, tok, tok, cache, cache, pl.BlockSpec(bias.shape, lambda b: (0, 0))],
        out_specs=tok,
        out_shape=jax.ShapeDtypeStruct((bsz, lq, D_ATTN), F32),
        compiler_params=_cparams("parallel"),
        name="attn_sample",
    )(per_seq(q), per_seq(k_new), per_seq(v_new), cache_kt, cache_vt, bias)
    return o.reshape(bsz * lq, D_ATTN)


def _ssd_kernel(*refs, rows, has_state):
    (xs_ref, bm_ref, cm_ref, dt_ref, z_ref, past_ref) = refs[:6]
    refs = refs[6:]
    if has_state:
        h0_ref = refs[0]
        refs = refs[1:]
    (cw_ref, cb_ref, dtb_ref, a_ref, dsk_ref, nrm_ref, tri_ref, rep_ref, hmask_ref,
     y_ref, hout_ref, xpad_ref, xc_ref) = refs
    c = pl.program_id(1)

    @pl.when(c == 0)
    def _():
        xpad_ref[0:SUBLANES, :] = past_ref[0]
        if has_state:
            hout_ref[0] = h0_ref[0]
        else:
            hout_ref[0] = jnp.zeros((D_SSM, D_STATE), F32)
        if rows < CHUNK:
            xpad_ref[SUBLANES + rows:SUBLANES + CHUNK, :] = jnp.zeros((CHUNK - rows, D_CONV), F32)

    @pl.when(c > 0)
    def _():
        xpad_ref[0:SUBLANES, :] = xpad_ref[CHUNK:CHUNK + SUBLANES, :]

    xpad_ref[SUBLANES:SUBLANES + rows, 0:D_SSM] = xs_ref[0]
    xpad_ref[SUBLANES:SUBLANES + rows, D_SSM:D_SSM + D_BC] = bm_ref[0]
    xpad_ref[SUBLANES:SUBLANES + rows, D_SSM + D_BC:D_CONV] = cm_ref[0]

    slab = 512
    for j in range(D_CONV // slab):
        cs = slice(j * slab, (j + 1) * slab)
        xp = xpad_ref[:, cs]
        acc = cb_ref[:, cs] + xp[SUBLANES:] * cw_ref[CONV_WIDTH - 1:CONV_WIDTH, cs]
        for shift in range(1, CONV_WIDTH):
            tap = CONV_WIDTH - 1 - shift
            acc = acc + pltpu.roll(xp, shift, axis=0)[SUBLANES:] * cw_ref[tap:tap + 1, cs]
        xc_ref[:, cs] = _silu(acc)

    dt_raw = dt_ref[0]
    if rows < CHUNK:
        dt_raw = jnp.concatenate([dt_raw, jnp.zeros((CHUNK - rows, LANES), F32)], axis=0)
    pre = dt_raw + dtb_ref[...]
    dt = jnp.maximum(pre, 0.0) + jnp.log(1.0 + jnp.exp(-jnp.abs(pre)))
    if rows < CHUNK:
        rid = lax.broadcasted_iota(jnp.int32, (CHUNK, LANES), 0)
        dt = jnp.where(rid < rows, dt, 0.0)
    da = dt * a_ref[...]
    acum = jnp.dot(tri_ref[...], da, preferred_element_type=F32, precision=lax.Precision.HIGHEST)
    acum_t = acum.T
    last = acum[CHUNK - 1:CHUNK, :]
    e_acum = jnp.exp(acum)
    w_coef = dt * jnp.exp(last - acum)

    ri = lax.broadcasted_iota(jnp.int32, (CHUNK, CHUNK), 0)
    ci = lax.broadcasted_iota(jnp.int32, (CHUNK, CHUNK), 1)
    causal = ri >= ci
    chunk_decay = jnp.exp(last)

    coef = jnp.concatenate([dt, e_acum, w_coef], axis=0)
    hi = coef.astype(BF16)
    lo = (coef - hi.astype(F32)).astype(BF16)
    wide = _dot(jnp.concatenate([hi, lo], axis=1), rep_ref[...])
    dt_w, e_w, w_w = wide[0:CHUNK], wide[CHUNK:2 * CHUNK], wide[2 * CHUNK:3 * CHUNK]

    for g in range(SSM_GROUPS):
        gs = slice(g * GROUP_WIDTH, (g + 1) * GROUP_WIDTH)
        bg = xc_ref[:, D_SSM + g * D_STATE:D_SSM + (g + 1) * D_STATE].astype(BF16)
        cg = xc_ref[:, D_SSM + D_BC + g * D_STATE:D_SSM + D_BC + (g + 1) * D_STATE].astype(BF16)
        xg = xc_ref[:, gs]
        cb = _dot_nt(cg, bg)
        xdt = (xg * dt_w[:, gs]).astype(BF16)
        m_parts, x_parts = [], []
        for k in range(HEADS_PER_GROUP):
            h = HEADS_PER_GROUP * g + k
            seg = acum[:, h:h + 1] - acum_t[h:h + 1, :]
            decay = jnp.exp(jnp.where(causal, seg, NEG))
            m_parts.append((cb * decay).astype(BF16))
            x_parts.append(xdt * hmask_ref[k])
        y = _dot(jnp.concatenate(m_parts, axis=1), jnp.concatenate(x_parts, axis=0))
        hg = hout_ref[0, gs, :]
        y = y + _dot_nt(cg, hg.astype(BF16)) * e_w[:, gs]
        y = y + dsk_ref[:, gs] * xg
        upd = _dot((xg * w_w[:, gs]).T.astype(BF16), bg)
        for k in range(HEADS_PER_GROUP):
            h = HEADS_PER_GROUP * g + k
            hs = slice(k * SSM_HEAD_DIM, (k + 1) * SSM_HEAD_DIM)
            hout_ref[0, h * SSM_HEAD_DIM:(h + 1) * SSM_HEAD_DIM, :] = chunk_decay[:, h:h + 1] * hg[hs] + upd[hs]
        yz = y[:rows] * _silu(z_ref[0, :, gs])
        ms = jnp.mean(yz * yz, axis=1, keepdims=True)
        y_ref[0, :, gs] = (yz * lax.rsqrt(ms + 1e-5) * nrm_ref[:, gs]).astype(y_ref.dtype)


def _ssd(xs, bm, cm, dt, z, past8, h0, wts, bsz, seq):
    rows = CHUNK if seq % CHUNK == 0 else seq
    nc = seq // rows
    tok = lambda n: pl.BlockSpec((1, rows, n), lambda b, c: (b * nc + c, 0, 0))
    full2 = lambda a: pl.BlockSpec(a.shape, lambda b, c: (0,) * a.ndim)
    state = pl.BlockSpec((1, D_SSM, D_STATE), lambda b, c: (b, 0, 0))
    has_state = h0 is not None
    chunks = lambda a: a.reshape(bsz * nc, rows, a.shape[-1])
    ins = ([chunks(a) for a in (xs, bm, cm, dt, z)] + [past8] + ([h0] if has_state else []) + list(wts))
    in_specs = ([tok(D_SSM), tok(D_BC), tok(D_BC), tok(LANES), tok(D_SSM),
                 pl.BlockSpec((1, SUBLANES, D_CONV), lambda b, c: (b, 0, 0))]
                + ([state] if has_state else []) + [full2(w) for w in wts])
    y, h_new = pl.pallas_call(
        functools.partial(_ssd_kernel, rows=rows, has_state=has_state),
        grid=(bsz, nc),
        in_specs=in_specs,
        out_specs=[tok(D_SSM), state],
        out_shape=[jax.ShapeDtypeStruct((bsz * nc, rows, D_SSM), BF16),
                   jax.ShapeDtypeStruct((bsz, D_SSM, D_STATE), F32)],
        scratch_shapes=[pltpu.VMEM((SUBLANES + CHUNK, D_CONV), F32),
                        pltpu.VMEM((CHUNK, D_CONV), F32)],
        compiler_params=_cparams("parallel", "arbitrary"),
        name="ssd",
    )(*ins)
    return y.reshape(bsz * seq, D_SSM), h_new


def _merge_kernel(o_ref, g_ref, ys_ref, ga_ref, gb_ref, x_ref, wa_ref, wb_ref, wo_ref, fn_ref, y_ref):
    ya = (o_ref[...] * _silu(g_ref[...])).astype(BF16)
    merged = (_sigmoid(ga_ref[...]) * _dot(ya, wa_ref[...])
              + _sigmoid(gb_ref[...]) * _dot(ys_ref[...], wb_ref[...]))
    out = x_ref[...] + _dot(merged.astype(BF16), wo_ref[...])
    ms = jnp.mean(out * out, axis=-1, keepdims=True)
    y_ref[...] = out * lax.rsqrt(ms + 1e-6) * fn_ref[...]


def _merge(o, g_attn, ys, gate_a, gate_b, x, wa, wb, wo, fnorm):
    t = x.shape[0]
    tm = _row_tile(t)
    row = lambda n: pl.BlockSpec((tm, n), lambda i: (i, 0))
    full = lambda a: pl.BlockSpec(a.shape, lambda i: (0, 0))
    return pl.pallas_call(
        _merge_kernel,
        grid=(t // tm,),
        in_specs=[row(D_ATTN), row(D_ATTN), row(D_SSM), row(D_MODEL), row(D_MODEL), row(D_MODEL),
                  full(wa), full(wb), full(wo), full(fnorm)],
        out_specs=row(D_MODEL),
        out_shape=jax.ShapeDtypeStruct((t, D_MODEL), F32),
        compiler_params=_cparams("parallel"),
        name="merge",
    )(o, g_attn, ys, gate_a, gate_b, x, wa, wb, wo, fnorm)


def _prep_weights(norm_g, w_in, conv_w, conv_b, dt_bias, a_log, d_skip, ssm_norm,
                  w_branch_a, w_branch_b, w_out, final_norm):
    sizes = (D_ATTN, D_ATTN, D_ATTN, D_ATTN, D_SSM, D_CONV, N_SSM_HEADS, D_MODEL, D_MODEL)
    pts = np.cumsum((0,) + sizes)
    seg = lambda i: w_in[:, pts[i]:pts[i + 1]]
    pad_heads = lambda v: jnp.pad(v.astype(F32), (0, LANES - N_SSM_HEADS)).reshape(1, LANES)
    w_a = w_in[:, 0:pts[4]].astype(BF16)
    w_b = jnp.concatenate([seg(4), seg(7), seg(8)], axis=1).astype(BF16)
    w_c = jnp.concatenate([seg(5), jnp.pad(seg(6), ((0, 0), (0, LANES - N_SSM_HEADS)))], axis=1).astype(BF16)
    tri = jnp.asarray(np.tril(np.ones((CHUNK, CHUNK), np.float32)))
    head_of = np.arange(D_SSM) // SSM_HEAD_DIM
    rep = (np.arange(LANES)[:, None] == head_of[None, :]).astype(np.float32)
    rep2 = jnp.asarray(np.concatenate([rep, rep], axis=0)).astype(BF16)
    hmask = jnp.asarray(np.broadcast_to(
        (np.arange(HEADS_PER_GROUP)[:, None, None] == head_of[None, None, :GROUP_WIDTH]),
        (HEADS_PER_GROUP, CHUNK, GROUP_WIDTH)).astype(np.float32)).astype(BF16)
    ssd_w = (conv_w.astype(F32), conv_b.astype(F32).reshape(1, D_CONV), pad_heads(dt_bias),
             pad_heads(-jnp.exp(a_log.astype(F32))),
             jnp.repeat(d_skip.astype(F32), SSM_HEAD_DIM).reshape(1, D_SSM),
             ssm_norm.astype(F32).reshape(1, D_SSM), tri, rep2, hmask)
    return dict(norm_g=norm_g.astype(F32).reshape(1, D_MODEL), w_a=w_a, w_b=w_b, w_c=w_c, ssd_w=ssd_w,
                wa=w_branch_a.astype(BF16), wb=w_branch_b.astype(BF16),
                wo=w_out.astype(BF16), fnorm=final_norm.astype(F32).reshape(1, D_MODEL))


def _layer(x, cache_k, cache_v, conv_past, ssm_past, wts, rel_bias):
    bsz, seq, _ = x.shape
    t = bsz * seq
    x2 = x.reshape(t, D_MODEL)
    h, q, k, v, g_attn, *kv_t = _proj_a(x2, wts["norm_g"], wts["w_a"], bsz, seq)
    z, gate_a, gate_b = _proj(h, wts["w_b"], (D_SSM, D_MODEL, D_MODEL), "proj_b")
    xs, bm, cm, dt = _proj(h, wts["w_c"], (D_SSM, D_BC, D_BC, LANES), "proj_c")

    if cache_k is None:
        o = _attn_prompt(q, k, v, _window_bias(rel_bias), bsz, seq)
        past8 = jnp.zeros((bsz, SUBLANES, D_CONV), F32)
    else:
        buf = cache_k.shape[1]
        feature_major = lambda a: jnp.transpose(a, (0, 2, 3, 1)).reshape(bsz, D_ATTN, buf)
        o = _attn_sample(q, k, v, feature_major(cache_k), feature_major(cache_v),
                         _sample_bias(rel_bias, seq, buf), bsz, seq)
        past8 = jnp.pad(conv_past.astype(F32), ((0, 0), (SUBLANES - (CONV_WIDTH - 1), 0), (0, 0)))

    h0 = None if ssm_past is None else ssm_past.astype(F32).reshape(bsz, D_SSM, D_STATE)
    ys, ssm_new = _ssd(xs, bm, cm, dt, z, past8, h0, wts["ssd_w"], bsz, seq)
    y = _merge(o, g_attn, ys, gate_a, gate_b, x2, wts["wa"], wts["wb"], wts["wo"], wts["fnorm"])

    tail = lambda a: a.reshape(bsz, seq, -1)[:, seq - (CONV_WIDTH - 1):]
    conv_new = jnp.concatenate([tail(xs), tail(bm), tail(cm)], axis=-1)
    if seq < CONV_WIDTH - 1:
        conv_new = jnp.concatenate([conv_past, conv_new], axis=1)[:, -(CONV_WIDTH - 1):]
    if kv_t:
        per_head = lambda a: jnp.transpose(a.reshape(bsz, N_HEADS_A, HEAD_DIM_A, seq), (0, 3, 1, 2))
        k_out, v_out = per_head(kv_t[0]), per_head(kv_t[1])
    else:
        k_out = k.reshape(bsz, seq, N_HEADS_A, HEAD_DIM_A)
        v_out = v.reshape(bsz, seq, N_HEADS_A, HEAD_DIM_A)
    return (y.reshape(bsz, seq, D_MODEL), k_out, v_out, conv_new,
            ssm_new.reshape(bsz, N_SSM_HEADS, SSM_HEAD_DIM, D_STATE))


def kernel(x_prompt, x_sample, cache_k, cache_v, state_conv, state_ssm, norm_g, w_in, conv_w, conv_b,
           dt_bias, a_log, d_skip, ssm_norm, w_branch_a, w_branch_b, w_out, rel_bias, final_norm):
    assert w_in.shape[0] == 1, "single layer"
    wts = _prep_weights(norm_g[0], w_in[0], conv_w[0], conv_b[0], dt_bias[0], a_log[0], d_skip[0],
                        ssm_norm[0], w_branch_a[0], w_branch_b[0], w_out[0], final_norm)
    keep = min(WINDOW_MAX, x_prompt.shape[1])
    yp, kp, vp, cp, sp = _layer(x_prompt, None, None, None, None, wts, rel_bias)
    ys, ks, vs, cs, ss = _layer(x_sample, cache_k[0], cache_v[0], state_conv[0], state_ssm[0], wts, rel_bias)
    return (yp, ys, kp[:, -keep:][None], vp[:, -keep:][None], cp[None], sp[None],
            ks[None], vs[None], cs[None], ss[None])
```

```python
import functools
import math

import numpy as np
import jax
import jax.numpy as jnp
from jax import lax
from jax.experimental import pallas as pl
from jax.experimental.pallas import tpu as pltpu

F32 = jnp.float32
BF16 = jnp.bfloat16

D_MODEL = 1024
N_HEADS_A = 12
HEAD_DIM_A = 64
D_ATTN = N_HEADS_A * HEAD_DIM_A
DILATIONS = (1, 4, 16)
WINDOW_KEYS = 128
WINDOW_MAX = 2048
N_BUCKETS = 32
MAX_DISTANCE = WINDOW_MAX
D_SSM = 2048
SSM_HEAD_DIM = 64
N_SSM_HEADS = D_SSM // SSM_HEAD_DIM
SSM_GROUPS = 8
HEADS_PER_GROUP = N_SSM_HEADS // SSM_GROUPS
GROUP_WIDTH = D_SSM // SSM_GROUPS
D_STATE = 128
CONV_WIDTH = 4
D_BC = SSM_GROUPS * D_STATE
D_CONV = D_SSM + 2 * D_BC
CHUNK = 128
LANES = 128
SUBLANES = 8
NEG = -1e30
BLOCKS_PER_STEP = 8
LOG2E = math.log2(math.e)
VMEM_LIMIT = 48 * 1024 * 1024


def _cparams(*sem):
    return pltpu.CompilerParams(dimension_semantics=sem, vmem_limit_bytes=VMEM_LIMIT)


def _dot(a, b):
    return jnp.dot(a, b, preferred_element_type=F32)


def _dot_nt(a, b):
    return lax.dot_general(a, b, (((1,), (1,)), ((), ())), preferred_element_type=F32)


def _silu(x):
    return x * (1.0 / (1.0 + jnp.exp(-x)))


def _sigmoid(x):
    return 1.0 / (1.0 + jnp.exp(-x))


def _div_pow2(x, n):
    assert n & (n - 1) == 0
    return jnp.right_shift(x, int(math.log2(n)))


def _proj_a_kernel(x_ref, g_ref, w_ref, h_ref, q_ref, k_ref, v_ref, ga_ref, *t_refs):
    xf = x_ref[...]
    ms = jnp.mean(xf * xf, axis=-1, keepdims=True)
    h = (xf * lax.rsqrt(ms + 1e-6) * g_ref[...]).astype(BF16)
    h_ref[...] = h
    q_ref[...] = _dot(h, w_ref[:, 0:D_ATTN]) * (LOG2E / math.sqrt(HEAD_DIM_A))
    k = _dot(h, w_ref[:, D_ATTN:2 * D_ATTN])
    k_ref[...] = k
    v = _dot(h, w_ref[:, 2 * D_ATTN:3 * D_ATTN])
    v_ref[...] = v
    ga_ref[...] = _dot(h, w_ref[:, 3 * D_ATTN:4 * D_ATTN])
    if t_refs:
        kt_ref, vt_ref = t_refs
        kt_ref[0] = k.T
        vt_ref[0] = v.T


def _proj_kernel(h_ref, w_ref, *out_refs, widths):
    h = h_ref[...]
    c0 = 0
    for o_ref, wd in zip(out_refs, widths):
        o_ref[...] = _dot(h, w_ref[:, c0:c0 + wd])
        c0 += wd


def _row_tile(t):
    return 512 if t % 512 == 0 else t


def _proj_a(x, norm_g, w_a, bsz, seq):
    t = x.shape[0]
    tm = _row_tile(t)
    row = lambda n: pl.BlockSpec((tm, n), lambda i: (i, 0))
    full = lambda a: pl.BlockSpec(a.shape, lambda i: (0, 0))
    outs = [jax.ShapeDtypeStruct((t, D_MODEL), BF16)] + [jax.ShapeDtypeStruct((t, D_ATTN), F32)] * 4
    out_specs = [row(s.shape[1]) for s in outs]
    if seq % tm == 0:
        per_seq = seq // tm
        outs += [jax.ShapeDtypeStruct((bsz, D_ATTN, seq), F32)] * 2
        out_specs += [pl.BlockSpec((1, D_ATTN, tm), lambda i: (i // per_seq, 0, i % per_seq))] * 2
    return pl.pallas_call(
        _proj_a_kernel,
        grid=(t // tm,),
        in_specs=[row(D_MODEL), full(norm_g), full(w_a)],
        out_specs=out_specs,
        out_shape=outs,
        compiler_params=_cparams("parallel"),
        name="proj_a",
    )(x, norm_g, w_a)


def _proj(h, w, widths, name):
    t = h.shape[0]
    tm = _row_tile(t)
    row = lambda n: pl.BlockSpec((tm, n), lambda i: (i, 0))
    outs = [jax.ShapeDtypeStruct((t, wd), F32) for wd in widths]
    return pl.pallas_call(
        functools.partial(_proj_kernel, widths=widths),
        grid=(t // tm,),
        in_specs=[row(D_MODEL), pl.BlockSpec(w.shape, lambda i: (0, 0))],
        out_specs=[row(wd) for wd in widths],
        out_shape=outs,
        compiler_params=_cparams("parallel"),
        name=name,
    )(h, w)


def _t5_bucket(dist):
    max_exact = N_BUCKETS // 2
    d = np.maximum(dist, 1).astype(np.float32)
    large = max_exact + (np.log(d / max_exact) / math.log(MAX_DISTANCE / max_exact)
                         * (N_BUCKETS - max_exact)).astype(np.int32)
    large = np.minimum(large, N_BUCKETS - 1)
    return np.where(dist < max_exact, dist, large).astype(np.int32)


def _bias_by_distance(rel_bias, dist, valid, extra=None):
    onehot = (_t5_bucket(np.maximum(dist, 0))[:, None] == np.arange(N_BUCKETS)[None, :]) & valid[:, None]
    b = jnp.dot(jnp.asarray(onehot.astype(np.float32)), rel_bias.astype(F32), precision=lax.Precision.HIGHEST)
    if extra is not None:
        b = b + jnp.asarray(extra.astype(np.float32))[:, None]
    return jnp.where(jnp.asarray(valid)[:, None], b * LOG2E, NEG)


def _window_bias(rel_bias):
    u = np.arange(2 * CHUNK)
    has_prev = jnp.asarray(np.arange(2 * CHUNK) >= CHUNK)
    out = []
    for d in DILATIONS:
        g = _bias_by_distance(rel_bias, (WINDOW_KEYS - u) * d, u <= WINDOW_KEYS).T
        reps = jnp.tile(g, (1, CHUNK * 2))[:, :CHUNK * (4 * CHUNK - 1)]
        toep = reps.reshape(N_HEADS_A, CHUNK, 4 * CHUNK - 1)[:, :, :2 * CHUNK]
        toep = toep.reshape(N_HEADS_A // 2, 2 * CHUNK, 2 * CHUNK)
        out.append(jnp.stack([toep, jnp.where(has_prev, toep, NEG)], axis=1))
    return jnp.stack(out)


def _sample_bias(rel_bias, lq, buf):
    ncol = buf + CHUNK
    dist = np.arange(-(CHUNK + lq), buf + lq)
    mult = np.zeros(dist.shape, np.int32)
    for d in DILATIONS:
        mult += ((dist >= 0) & (dist % d == 0) & (dist <= WINDOW_KEYS * d)).astype(np.int32)
    f = _bias_by_distance(rel_bias, dist, mult > 0, np.log(np.maximum(mult, 1))).T
    fr = f[:, ::-1]
    rows = [fr[:, lq - 1 - i:lq - 1 - i + ncol] for i in range(lq)]
    return jnp.stack(rows, axis=1).reshape(N_HEADS_A * lq, ncol)


def _attn_prompt_kernel(q_ref, k_ref, v_ref, bias_ref, o_ref, m_ref, l_ref,
                        q4_ref, k4_ref, v4_ref, o4_ref, m4_ref, l4_ref, *, seq):
    s1 = DILATIONS[1]
    assert DILATIONS == (1, s1, s1 * s1) and seq // DILATIONS[2] == CHUNK
    lc4 = seq // s1
    lane = lax.broadcasted_iota(jnp.int32, (CHUNK, LANES), 1)
    first_half = lane < HEAD_DIM_A

    def rows(ref, start, stride):
        if stride == 1:
            return ref[0, pl.ds(start, CHUNK), :]
        return ref[0, pl.ds(start, CHUNK, stride=stride), :]

    def put(ref, start, stride, val):
        if stride == 1:
            ref[0, pl.ds(start, CHUNK), :] = val
        else:
            ref[0, pl.ds(start, CHUNK, stride=stride), :] = val

    def pair_tile(x):
        return jnp.where(first_half, x[:CHUNK], x[CHUNK:])

    def scores(gi, src, stride, start, prev_start, first):
        qr, kr, vr = src
        qp = rows(qr, start, stride)
        qs = jnp.concatenate([jnp.where(first_half, qp, 0.0),
                              jnp.where(first_half, 0.0, qp)], axis=0).astype(BF16)
        kw = rows(kr, start, stride).astype(BF16)
        vw = rows(vr, start, stride).astype(BF16)
        if prev_start is None:
            s = _dot_nt(qs, kw) + bias_ref[gi, 0, 0, :, CHUNK:]
        else:
            kw = jnp.concatenate([rows(kr, prev_start, stride).astype(BF16), kw], axis=0)
            vw = jnp.concatenate([rows(vr, prev_start, stride).astype(BF16), vw], axis=0)
            s = _dot_nt(qs, kw) + bias_ref[gi, 0, first]
        m = jnp.max(s, axis=1, keepdims=True)
        e = jnp.exp2(s - m)
        l = pair_tile(jnp.sum(e, axis=1, keepdims=True))
        pv = pair_tile(_dot(e.astype(BF16), vw))
        return pv, pair_tile(m), l

    def accumulate(acc, stride, start, pv, m, l, fresh, final):
        o_acc, m_acc, l_acc = acc
        if not fresh:
            m_old = rows(m_acc, start, stride)
            m_new = jnp.maximum(m_old, m)
            a = jnp.exp2(m_old - m_new)
            b = jnp.exp2(m - m_new)
            pv = a * rows(o_acc, start, stride) + b * pv
            l = a * rows(l_acc, start, stride) + b * l
            m = m_new
        if final:
            put(o_acc, start, stride, pv * (1.0 / l))
        else:
            put(o_acc, start, stride, pv)
            put(m_acc, start, stride, m)
            put(l_acc, start, stride, l)

    def group(gi, src, acc, stride, n_blocks, locate, fresh=False, final=False):
        def body(it, carry):
            starts, parts = [], []
            for u in range(BLOCKS_PER_STEP):
                start, prev, first = locate(it * BLOCKS_PER_STEP + u)
                starts.append(start)
                parts.append(scores(gi, src, stride, start, prev, first))
            for start, part in zip(starts, parts):
                accumulate(acc, stride, start, *part, fresh, final)
            return carry

        lax.fori_loop(0, n_blocks // BLOCKS_PER_STEP, body, 0)

    def windowed(blk, blocks_per_class):
        i = blk % blocks_per_class
        start = pl.multiple_of(blk * CHUNK, CHUNK)
        prev = pl.multiple_of((blk - jnp.minimum(i, 1)) * CHUNK, CHUNK)
        return start, prev, jnp.where(i == 0, 1, 0)

    def class16(blk):
        return (blk % s1) * lc4 + blk // s1, None, None

    for src, dst in ((q_ref, q4_ref), (k_ref, k4_ref), (v_ref, v4_ref)):
        for r in range(s1):
            dst[0, r * lc4:(r + 1) * lc4, :] = src[0, pl.ds(r, lc4, stride=s1), :]

    token_order = (q_ref, k_ref, v_ref), (o_ref, m_ref, l_ref)
    class_order = (q4_ref, k4_ref, v4_ref), (o4_ref, m4_ref, l4_ref)
    n_blocks = seq // CHUNK
    group(1, *class_order, 1, n_blocks, lambda blk: windowed(blk, lc4 // CHUNK), fresh=True)
    group(2, *class_order, s1, n_blocks, class16)
    for src, dst in zip(class_order[1], token_order[1]):
        for r in range(s1):
            dst[0, pl.ds(r, lc4, stride=s1), :] = src[0, r * lc4:(r + 1) * lc4, :]
    group(0, *token_order, 1, n_blocks, lambda blk: windowed(blk, n_blocks), final=True)


def _attn_prompt(q, k, v, bias, bsz, seq):
    assert all(seq % (d * CHUNK) == 0 for d in DILATIONS)
    pairs = N_HEADS_A // 2
    view = lambda a: a.reshape(bsz, seq, D_ATTN)
    blk = pl.BlockSpec((1, seq, LANES), lambda b, p: (b, 0, p))
    o = pl.pallas_call(
        functools.partial(_attn_prompt_kernel, seq=seq),
        grid=(bsz, pairs),
        in_specs=[blk, blk, blk,
                  pl.BlockSpec((len(DILATIONS), 1, 2, 2 * CHUNK, 2 * CHUNK), lambda b, p: (0, p, 0, 0, 0))],
        out_specs=blk,
        out_shape=jax.ShapeDtypeStruct((bsz, seq, D_ATTN), F32),
        scratch_shapes=[pltpu.VMEM((1, seq, LANES), F32)] * 8,
        compiler_params=_cparams("parallel", "parallel"),
        name="attn_prompt",
    )(view(q), view(k), view(v), bias)
    return o.reshape(bsz * seq, D_ATTN)


def _attn_sample_kernel(q_ref, kn_ref, vn_ref, ck_ref, cv_ref, bias_ref, o_ref, *, lq, buf):
    nrow = N_HEADS_A * lq
    pad = jnp.zeros((CHUNK - lq, D_ATTN), F32)
    row_head = _div_pow2(lax.broadcasted_iota(jnp.int32, (nrow, D_ATTN), 0), lq)
    col_head = _div_pow2(lax.broadcasted_iota(jnp.int32, (nrow, D_ATTN), 1), HEAD_DIM_A)
    own = row_head == col_head
    qs = jnp.where(own, jnp.concatenate([q_ref[0]] * N_HEADS_A, axis=0), 0.0).astype(BF16)
    kn = jnp.concatenate([kn_ref[0], pad], axis=0).astype(BF16)
    vn = jnp.concatenate([vn_ref[0], pad], axis=0).astype(BF16)
    s_c = _dot(qs, ck_ref[0].astype(BF16)) + bias_ref[:, 0:buf]
    s_n = _dot_nt(qs, kn) + bias_ref[:, buf:buf + CHUNK]
    m = jnp.maximum(jnp.max(s_c, axis=1, keepdims=True), jnp.max(s_n, axis=1, keepdims=True))
    e_c = jnp.exp2(s_c - m)
    e_n = jnp.exp2(s_n - m)
    l = jnp.sum(e_c, axis=1, keepdims=True) + jnp.sum(e_n, axis=1, keepdims=True)
    pv = _dot_nt(e_c.astype(BF16), cv_ref[0].astype(BF16)) + _dot(e_n.astype(BF16), vn)
    pv = jnp.where(own, pv * (1.0 / l), 0.0)
    o = pv[0:lq]
    for h in range(1, N_HEADS_A):
        o = o + pv[h * lq:(h + 1) * lq]
    o_ref[0] = o


def _attn_sample(q, k_new, v_new, cache_kt, cache_vt, bias, bsz, lq):
    buf = cache_kt.shape[2]
    assert lq & (lq - 1) == 0 and lq <= CHUNK
    tok = pl.BlockSpec((1, lq, D_ATTN), lambda b: (b, 0, 0))
    cache = pl.BlockSpec((1, D_ATTN, buf), lambda b: (b, 0, 0))
    per_seq = lambda a: a.reshape(bsz, lq, D_ATTN)
    o = pl.pallas_call(
        functools.partial(_attn_sample_kernel, lq=lq, buf=buf),
        grid=(bsz,),
        in_specs=[tok, tok, tok, cache, cache, pl.BlockSpec(bias.shape, lambda b: (0, 0))],
        out_specs=tok,
        out_shape=jax.ShapeDtypeStruct((bsz, lq, D_ATTN), F32),
        compiler_params=_cparams("parallel"),
        name="attn_sample",
    )(per_seq(q), per_seq(k_new), per_seq(v_new), cache_kt, cache_vt, bias)
    return o.reshape(bsz * lq, D_ATTN)


def _ssd_kernel(*refs, rows, has_state):
    (xs_ref, bm_ref, cm_ref, dt_ref, z_ref, past_ref) = refs[:6]
    refs = refs[6:]
    if has_state:
        h0_ref = refs[0]
        refs = refs[1:]
    (cw_ref, cb_ref, dtb_ref, a_ref, dsk_ref, nrm_ref, tri_ref, rep_ref, hmask_ref,
     y_ref, hout_ref, xpad_ref, xc_ref) = refs
    c = pl.program_id(1)

    @pl.when(c == 0)
    def _():
        xpad_ref[0:SUBLANES, :] = past_ref[0]
        if has_state:
            hout_ref[0] = h0_ref[0]
        else:
            hout_ref[0] = jnp.zeros((D_SSM, D_STATE), F32)
        if rows < CHUNK:
            xpad_ref[SUBLANES + rows:SUBLANES + CHUNK, :] = jnp.zeros((CHUNK - rows, D_CONV), F32)

    @pl.when(c > 0)
    def _():
        xpad_ref[0:SUBLANES, :] = xpad_ref[CHUNK:CHUNK + SUBLANES, :]

    xpad_ref[SUBLANES:SUBLANES + rows, 0:D_SSM] = xs_ref[0]
    xpad_ref[SUBLANES:SUBLANES + rows, D_SSM:D_SSM + D_BC] = bm_ref[0]
    xpad_ref[SUBLANES:SUBLANES + rows, D_SSM + D_BC:D_CONV] = cm_ref[0]

    slab = 512
    for j in range(D_CONV // slab):
        cs = slice(j * slab, (j + 1) * slab)
        xp = xpad_ref[:, cs]
        acc = cb_ref[:, cs] + xp[SUBLANES:] * cw_ref[CONV_WIDTH - 1:CONV_WIDTH, cs]
        for shift in range(1, CONV_WIDTH):
            tap = CONV_WIDTH - 1 - shift
            acc = acc + pltpu.roll(xp, shift, axis=0)[SUBLANES:] * cw_ref[tap:tap + 1, cs]
        xc_ref[:, cs] = _silu(acc)

    dt_raw = dt_ref[0]
    if rows < CHUNK:
        dt_raw = jnp.concatenate([dt_raw, jnp.zeros((CHUNK - rows, LANES), F32)], axis=0)
    pre = dt_raw + dtb_ref[...]
    dt = jnp.maximum(pre, 0.0) + jnp.log(1.0 + jnp.exp(-jnp.abs(pre)))
    if rows < CHUNK:
        rid = lax.broadcasted_iota(jnp.int32, (CHUNK, LANES), 0)
        dt = jnp.where(rid < rows, dt, 0.0)
    da = dt * a_ref[...]
    acum = jnp.dot(tri_ref[...], da, preferred_element_type=F32, precision=lax.Precision.HIGHEST)
    acum_t = acum.T
    last = acum[CHUNK - 1:CHUNK, :]
    e_acum = jnp.exp(acum)
    w_coef = dt * jnp.exp(last - acum)

    ri = lax.broadcasted_iota(jnp.int32, (CHUNK, CHUNK), 0)
    ci = lax.broadcasted_iota(jnp.int32, (CHUNK, CHUNK), 1)
    causal = ri >= ci
    chunk_decay = jnp.exp(last)

    coef = jnp.concatenate([dt, e_acum, w_coef], axis=0)
    hi = coef.astype(BF16)
    lo = (coef - hi.astype(F32)).astype(BF16)
    wide = _dot(jnp.concatenate([hi, lo], axis=1), rep_ref[...])
    dt_w, e_w, w_w = wide[0:CHUNK], wide[CHUNK:2 * CHUNK], wide[2 * CHUNK:3 * CHUNK]

    for g in range(SSM_GROUPS):
        gs = slice(g * GROUP_WIDTH, (g + 1) * GROUP_WIDTH)
        bg = xc_ref[:, D_SSM + g * D_STATE:D_SSM + (g + 1) * D_STATE].astype(BF16)
        cg = xc_ref[:, D_SSM + D_BC + g * D_STATE:D_SSM + D_BC + (g + 1) * D_STATE].astype(BF16)
        xg = xc_ref[:, gs]
        cb = _dot_nt(cg, bg)
        xdt = (xg * dt_w[:, gs]).astype(BF16)
        m_parts, x_parts = [], []
        for k in range(HEADS_PER_GROUP):
            h = HEADS_PER_GROUP * g + k
            seg = acum[:, h:h + 1] - acum_t[h:h + 1, :]
            decay = jnp.exp(jnp.where(causal, seg, NEG))
            m_parts.append((cb * decay).astype(BF16))
            x_parts.append(xdt * hmask_ref[k])
        y = _dot(jnp.concatenate(m_parts, axis=1), jnp.concatenate(x_parts, axis=0))
        hg = hout_ref[0, gs, :]
        y = y + _dot_nt(cg, hg.astype(BF16)) * e_w[:, gs]
        y = y + dsk_ref[:, gs] * xg
        upd = _dot((xg * w_w[:, gs]).T.astype(BF16), bg)
        for k in range(HEADS_PER_GROUP):
            h = HEADS_PER_GROUP * g + k
            hs = slice(k * SSM_HEAD_DIM, (k + 1) * SSM_HEAD_DIM)
            hout_ref[0, h * SSM_HEAD_DIM:(h + 1) * SSM_HEAD_DIM, :] = chunk_decay[:, h:h + 1] * hg[hs] + upd[hs]
        yz = y[:rows] * _silu(z_ref[0, :, gs])
        ms = jnp.mean(yz * yz, axis=1, keepdims=True)
        y_ref[0, :, gs] = (yz * lax.rsqrt(ms + 1e-5) * nrm_ref[:, gs]).astype(y_ref.dtype)


def _ssd(xs, bm, cm, dt, z, past8, h0, wts, bsz, seq):
    rows = CHUNK if seq % CHUNK == 0 else seq
    nc = seq // rows
    tok = lambda n: pl.BlockSpec((1, rows, n), lambda b, c: (b * nc + c, 0, 0))
    full2 = lambda a: pl.BlockSpec(a.shape, lambda b, c: (0,) * a.ndim)
    state = pl.BlockSpec((1, D_SSM, D_STATE), lambda b, c: (b, 0, 0))
    has_state = h0 is not None
    chunks = lambda a: a.reshape(bsz * nc, rows, a.shape[-1])
    ins = ([chunks(a) for a in (xs, bm, cm, dt, z)] + [past8] + ([h0] if has_state else []) + list(wts))
    in_specs = ([tok(D_SSM), tok(D_BC), tok(D_BC), tok(LANES), tok(D_SSM),
                 pl.BlockSpec((1, SUBLANES, D_CONV), lambda b, c: (b, 0, 0))]
                + ([state] if has_state else []) + [full2(w) for w in wts])
    y, h_new = pl.pallas_call(
        functools.partial(_ssd_kernel, rows=rows, has_state=has_state),
        grid=(bsz, nc),
        in_specs=in_specs,
        out_specs=[tok(D_SSM), state],
        out_shape=[jax.ShapeDtypeStruct((bsz * nc, rows, D_SSM), BF16),
                   jax.ShapeDtypeStruct((bsz, D_SSM, D_STATE), F32)],
        scratch_shapes=[pltpu.VMEM((SUBLANES + CHUNK, D_CONV), F32),
                        pltpu.VMEM((CHUNK, D_CONV), F32)],
        compiler_params=_cparams("parallel", "arbitrary"),
        name="ssd",
    )(*ins)
    return y.reshape(bsz * seq, D_SSM), h_new


def _merge_kernel(o_ref, g_ref, ys_ref, ga_ref, gb_ref, x_ref, wa_ref, wb_ref, wo_ref, fn_ref, y_ref):
    ya = (o_ref[...] * _silu(g_ref[...])).astype(BF16)
    merged = (_sigmoid(ga_ref[...]) * _dot(ya, wa_ref[...])
              + _sigmoid(gb_ref[...]) * _dot(ys_ref[...], wb_ref[...]))
    out = x_ref[...] + _dot(merged.astype(BF16), wo_ref[...])
    ms = jnp.mean(out * out, axis=-1, keepdims=True)
    y_ref[...] = out * lax.rsqrt(ms + 1e-6) * fn_ref[...]


def _merge(o, g_attn, ys, gate_a, gate_b, x, wa, wb, wo, fnorm):
    t = x.shape[0]
    tm = _row_tile(t)
    row = lambda n: pl.BlockSpec((tm, n), lambda i: (i, 0))
    full = lambda a: pl.BlockSpec(a.shape, lambda i: (0, 0))
    return pl.pallas_call(
        _merge_kernel,
        grid=(t // tm,),
        in_specs=[row(D_ATTN), row(D_ATTN), row(D_SSM), row(D_MODEL), row(D_MODEL), row(D_MODEL),
                  full(wa), full(wb), full(wo), full(fnorm)],
        out_specs=row(D_MODEL),
        out_shape=jax.ShapeDtypeStruct((t, D_MODEL), F32),
        compiler_params=_cparams("parallel"),
        name="merge",
    )(o, g_attn, ys, gate_a, gate_b, x, wa, wb, wo, fnorm)


def _prep_weights(norm_g, w_in, conv_w, conv_b, dt_bias, a_log, d_skip, ssm_norm,
                  w_branch_a, w_branch_b, w_out, final_norm):
    sizes = (D_ATTN, D_ATTN, D_ATTN, D_ATTN, D_SSM, D_CONV, N_SSM_HEADS, D_MODEL, D_MODEL)
    pts = np.cumsum((0,) + sizes)
    seg = lambda i: w_in[:, pts[i]:pts[i + 1]]
    pad_heads = lambda v: jnp.pad(v.astype(F32), (0, LANES - N_SSM_HEADS)).reshape(1, LANES)
    w_a = w_in[:, 0:pts[4]].astype(BF16)
    w_b = jnp.concatenate([seg(4), seg(7), seg(8)], axis=1).astype(BF16)
    w_c = jnp.concatenate([seg(5), jnp.pad(seg(6), ((0, 0), (0, LANES - N_SSM_HEADS)))], axis=1).astype(BF16)
    tri = jnp.asarray(np.tril(np.ones((CHUNK, CHUNK), np.float32)))
    head_of = np.arange(D_SSM) // SSM_HEAD_DIM
    rep = (np.arange(LANES)[:, None] == head_of[None, :]).astype(np.float32)
    rep2 = jnp.asarray(np.concatenate([rep, rep], axis=0)).astype(BF16)
    hmask = jnp.asarray(np.broadcast_to(
        (np.arange(HEADS_PER_GROUP)[:, None, None] == head_of[None, None, :GROUP_WIDTH]),
        (HEADS_PER_GROUP, CHUNK, GROUP_WIDTH)).astype(np.float32)).astype(BF16)
    ssd_w = (conv_w.astype(F32), conv_b.astype(F32).reshape(1, D_CONV), pad_heads(dt_bias),
             pad_heads(-jnp.exp(a_log.astype(F32))),
             jnp.repeat(d_skip.astype(F32), SSM_HEAD_DIM).reshape(1, D_SSM),
             ssm_norm.astype(F32).reshape(1, D_SSM), tri, rep2, hmask)
    return dict(norm_g=norm_g.astype(F32).reshape(1, D_MODEL), w_a=w_a, w_b=w_b, w_c=w_c, ssd_w=ssd_w,
                wa=w_branch_a.astype(BF16), wb=w_branch_b.astype(BF16),
                wo=w_out.astype(BF16), fnorm=final_norm.astype(F32).reshape(1, D_MODEL))


def _layer(x, cache_k, cache_v, conv_past, ssm_past, wts, rel_bias):
    bsz, seq, _ = x.shape
    t = bsz * seq
    x2 = x.reshape(t, D_MODEL)
    h, q, k, v, g_attn, *kv_t = _proj_a(x2, wts["norm_g"], wts["w_a"], bsz, seq)
    z, gate_a, gate_b = _proj(h, wts["w_b"], (D_SSM, D_MODEL, D_MODEL), "proj_b")
    xs, bm, cm, dt = _proj(h, wts["w_c"], (D_SSM, D_BC, D_BC, LANES), "proj_c")

    if cache_k is None:
        o = _attn_prompt(q, k, v, _window_bias(rel_bias), bsz, seq)
        past8 = jnp.zeros((bsz, SUBLANES, D_CONV), F32)
    else:
        buf = cache_k.shape[1]
        feature_major = lambda a: jnp.transpose(a, (0, 2, 3, 1)).reshape(bsz, D_ATTN, buf)
        o = _attn_sample(q, k, v, feature_major(cache_k), feature_major(cache_v),
                         _sample_bias(rel_bias, seq, buf), bsz, seq)
        past8 = jnp.pad(conv_past.astype(F32), ((0, 0), (SUBLANES - (CONV_WIDTH - 1), 0), (0, 0)))

    h0 = None if ssm_past is None else ssm_past.astype(F32).reshape(bsz, D_SSM, D_STATE)
    ys, ssm_new = _ssd(xs, bm, cm, dt, z, past8, h0, wts["ssd_w"], bsz, seq)
    y = _merge(o, g_attn, ys, gate_a, gate_b, x2, wts["wa"], wts["wb"], wts["wo"], wts["fnorm"])

    tail = lambda a: a.reshape(bsz, seq, -1)[:, seq - (CONV_WIDTH - 1):]
    conv_new = jnp.concatenate([tail(xs), tail(bm), tail(cm)], axis=-1)
    if seq < CONV_WIDTH - 1:
        conv_new = jnp.concatenate([conv_past, conv_new], axis=1)[:, -(CONV_WIDTH - 1):]
    if kv_t:
        per_head = lambda a: jnp.transpose(a.reshape(bsz, N_HEADS_A, HEAD_DIM_A, seq), (0, 3, 1, 2))
        k_out, v_out = per_head(kv_t[0]), per_head(kv_t[1])
    else:
        k_out = k.reshape(bsz, seq, N_HEADS_A, HEAD_DIM_A)
        v_out = v.reshape(bsz, seq, N_HEADS_A, HEAD_DIM_A)
    return (y.reshape(bsz, seq, D_MODEL), k_out, v_out, conv_new,
            ssm_new.reshape(bsz, N_SSM_HEADS, SSM_HEAD_DIM, D_STATE))


def kernel(x_prompt, x_sample, cache_k, cache_v, state_conv, state_ssm, norm_g, w_in, conv_w, conv_b,
           dt_bias, a_log, d_skip, ssm_norm, w_branch_a, w_branch_b, w_out, rel_bias, final_norm):
    assert w_in.shape[0] == 1, "single layer"
    wts = _prep_weights(norm_g[0], w_in[0], conv_w[0], conv_b[0], dt_bias[0], a_log[0], d_skip[0],
                        ssm_norm[0], w_branch_a[0], w_branch_b[0], w_out[0], final_norm)
    keep = min(WINDOW_MAX, x_prompt.shape[1])
    yp, kp, vp, cp, sp = _layer(x_prompt, None, None, None, None, wts, rel_bias)
    ys, ks, vs, cs, ss = _layer(x_sample, cache_k[0], cache_v[0], state_conv[0], state_ssm[0], wts, rel_bias)
    return (yp, ys, kp[:, -keep:][None], vp[:, -keep:][None], cp[None], sp[None],
            ks[None], vs[None], cs[None], ss[None])
```

```python
import functools
import math

import numpy as np
import jax
import jax.numpy as jnp
from jax import lax
from jax.experimental import pallas as pl
from jax.experimental.pallas import tpu as pltpu

F32 = jnp.float32
BF16 = jnp.bfloat16

D_MODEL = 1024
N_HEADS_A = 12
HEAD_DIM_A = 64
D_ATTN = N_HEADS_A * HEAD_DIM_A
DILATIONS = (1, 4, 16)
WINDOW_KEYS = 128
WINDOW_MAX = 2048
N_BUCKETS = 32
MAX_DISTANCE = WINDOW_MAX
D_SSM = 2048
SSM_HEAD_DIM = 64
N_SSM_HEADS = D_SSM // SSM_HEAD_DIM
SSM_GROUPS = 8
HEADS_PER_GROUP = N_SSM_HEADS // SSM_GROUPS
GROUP_WIDTH = D_SSM // SSM_GROUPS
D_STATE = 128
CONV_WIDTH = 4
D_BC = SSM_GROUPS * D_STATE
D_CONV = D_SSM + 2 * D_BC
CHUNK = 128
LANES = 128
SUBLANES = 8
NEG = -1e30
BLOCKS_PER_STEP = 8
LOG2E = math.log2(math.e)
VMEM_LIMIT = 48 * 1024 * 1024


def _cparams(*sem):
    return pltpu.CompilerParams(dimension_semantics=sem, vmem_limit_bytes=VMEM_LIMIT)


def _dot(a, b):
    return jnp.dot(a, b, preferred_element_type=F32)


def _dot_nt(a, b):
    return lax.dot_general(a, b, (((1,), (1,)), ((), ())), preferred_element_type=F32)


def _silu(x):
    return x * (1.0 / (1.0 + jnp.exp(-x)))


def _sigmoid(x):
    return 1.0 / (1.0 + jnp.exp(-x))


def _div_pow2(x, n):
    assert n & (n - 1) == 0
    return jnp.right_shift(x, int(math.log2(n)))


def _proj_a_kernel(x_ref, g_ref, w_ref, h_ref, q_ref, k_ref, v_ref, ga_ref, *t_refs):
    xf = x_ref[...]
    ms = jnp.mean(xf * xf, axis=-1, keepdims=True)
    h = (xf * lax.rsqrt(ms + 1e-6) * g_ref[...]).astype(BF16)
    h_ref[...] = h
    q_ref[...] = _dot(h, w_ref[:, 0:D_ATTN]) * (LOG2E / math.sqrt(HEAD_DIM_A))
    k = _dot(h, w_ref[:, D_ATTN:2 * D_ATTN])
    k_ref[...] = k
    v = _dot(h, w_ref[:, 2 * D_ATTN:3 * D_ATTN])
    v_ref[...] = v
    ga_ref[...] = _dot(h, w_ref[:, 3 * D_ATTN:4 * D_ATTN])
    if t_refs:
        kt_ref, vt_ref = t_refs
        kt_ref[0] = k.T
        vt_ref[0] = v.T


def _proj_kernel(h_ref, w_ref, *out_refs, widths):
    h = h_ref[...]
    c0 = 0
    for o_ref, wd in zip(out_refs, widths):
        o_ref[...] = _dot(h, w_ref[:, c0:c0 + wd])
        c0 += wd


def _row_tile(t):
    return 512 if t % 512 == 0 else t


def _proj_a(x, norm_g, w_a, bsz, seq):
    t = x.shape[0]
    tm = _row_tile(t)
    row = lambda n: pl.BlockSpec((tm, n), lambda i: (i, 0))
    full = lambda a: pl.BlockSpec(a.shape, lambda i: (0, 0))
    outs = [jax.ShapeDtypeStruct((t, D_MODEL), BF16)] + [jax.ShapeDtypeStruct((t, D_ATTN), F32)] * 4
    out_specs = [row(s.shape[1]) for s in outs]
    if seq % tm == 0:
        per_seq = seq // tm
        outs += [jax.ShapeDtypeStruct((bsz, D_ATTN, seq), F32)] * 2
        out_specs += [pl.BlockSpec((1, D_ATTN, tm), lambda i: (i // per_seq, 0, i % per_seq))] * 2
    return pl.pallas_call(
        _proj_a_kernel,
        grid=(t // tm,),
        in_specs=[row(D_MODEL), full(norm_g), full(w_a)],
        out_specs=out_specs,
        out_shape=outs,
        compiler_params=_cparams("parallel"),
        name="proj_a",
    )(x, norm_g, w_a)


def _proj(h, w, widths, name):
    t = h.shape[0]
    tm = _row_tile(t)
    row = lambda n: pl.BlockSpec((tm, n), lambda i: (i, 0))
    outs = [jax.ShapeDtypeStruct((t, wd), F32) for wd in widths]
    return pl.pallas_call(
        functools.partial(_proj_kernel, widths=widths),
        grid=(t // tm,),
        in_specs=[row(D_MODEL), pl.BlockSpec(w.shape, lambda i: (0, 0))],
        out_specs=[row(wd) for wd in widths],
        out_shape=outs,
        compiler_params=_cparams("parallel"),
        name=name,
    )(h, w)


def _t5_bucket(dist):
    max_exact = N_BUCKETS // 2
    d = np.maximum(dist, 1).astype(np.float32)
    large = max_exact + (np.log(d / max_exact) / math.log(MAX_DISTANCE / max_exact)
                         * (N_BUCKETS - max_exact)).astype(np.int32)
    large = np.minimum(large, N_BUCKETS - 1)
    return np.where(dist < max_exact, dist, large).astype(np.int32)


def _bias_by_distance(rel_bias, dist, valid, extra=None):
    onehot = (_t5_bucket(np.maximum(dist, 0))[:, None] == np.arange(N_BUCKETS)[None, :]) & valid[:, None]
    b = jnp.dot(jnp.asarray(onehot.astype(np.float32)), rel_bias.astype(F32), precision=lax.Precision.HIGHEST)
    if extra is not None:
        b = b + jnp.asarray(extra.astype(np.float32))[:, None]
    return jnp.where(jnp.asarray(valid)[:, None], b * LOG2E, NEG)


def _window_bias(rel_bias):
    u = np.arange(2 * CHUNK)
    has_prev = jnp.asarray(np.arange(2 * CHUNK) >= CHUNK)
    out = []
    for d in DILATIONS:
        g = _bias_by_distance(rel_bias, (WINDOW_KEYS - u) * d, u <= WINDOW_KEYS).T
        reps = jnp.tile(g, (1, CHUNK * 2))[:, :CHUNK * (4 * CHUNK - 1)]
        toep = reps.reshape(N_HEADS_A, CHUNK, 4 * CHUNK - 1)[:, :, :2 * CHUNK]
        toep = toep.reshape(N_HEADS_A // 2, 2 * CHUNK, 2 * CHUNK)
        out.append(jnp.stack([toep, jnp.where(has_prev, toep, NEG)], axis=1))
    return jnp.stack(out)


def _sample_bias(rel_bias, lq, buf):
    ncol = buf + CHUNK
    dist = np.arange(buf + lq - 1, -(CHUNK + lq) - 1, -1)
    mult = np.zeros(dist.shape, np.int32)
    for d in DILATIONS:
        mult += ((dist >= 0) & (dist % d == 0) & (dist <= WINDOW_KEYS * d)).astype(np.int32)
    fr = _bias_by_distance(rel_bias, dist, mult > 0, np.log(np.maximum(mult, 1))).T
    rows = [fr[:, lq - 1 - i:lq - 1 - i + ncol] for i in range(lq)]
    return jnp.stack(rows, axis=1).reshape(N_HEADS_A * lq, ncol)


def _attn_prompt_kernel(q_ref, k_ref, v_ref, bias_ref, o_ref, m_ref, l_ref,
                        q4_ref, k4_ref, v4_ref, o4_ref, m4_ref, l4_ref, *, seq):
    s1 = DILATIONS[1]
    assert DILATIONS == (1, s1, s1 * s1) and seq // DILATIONS[2] == CHUNK
    lc4 = seq // s1
    lane = lax.broadcasted_iota(jnp.int32, (CHUNK, LANES), 1)
    first_half = lane < HEAD_DIM_A

    def rows(ref, start, stride):
        if stride == 1:
            return ref[0, pl.ds(start, CHUNK), :]
        return ref[0, pl.ds(start, CHUNK, stride=stride), :]

    def put(ref, start, stride, val):
        if stride == 1:
            ref[0, pl.ds(start, CHUNK), :] = val
        else:
            ref[0, pl.ds(start, CHUNK, stride=stride), :] = val

    def pair_tile(x):
        return jnp.where(first_half, x[:CHUNK], x[CHUNK:])

    def scores(gi, src, stride, start, prev_start, first):
        qr, kr, vr = src
        qp = rows(qr, start, stride)
        qs = jnp.concatenate([jnp.where(first_half, qp, 0.0),
                              jnp.where(first_half, 0.0, qp)], axis=0).astype(BF16)
        kw = rows(kr, start, stride).astype(BF16)
        vw = rows(vr, start, stride).astype(BF16)
        if prev_start is None:
            s = _dot_nt(qs, kw) + bias_ref[gi, 0, 0, :, CHUNK:]
        else:
            kw = jnp.concatenate([rows(kr, prev_start, stride).astype(BF16), kw], axis=0)
            vw = jnp.concatenate([rows(vr, prev_start, stride).astype(BF16), vw], axis=0)
            s = _dot_nt(qs, kw) + bias_ref[gi, 0, first]
        m = jnp.max(s, axis=1, keepdims=True)
        e = jnp.exp2(s - m)
        l = pair_tile(jnp.sum(e, axis=1, keepdims=True))
        pv = pair_tile(_dot(e.astype(BF16), vw))
        return pv, pair_tile(m), l

    def accumulate(acc, stride, start, pv, m, l, fresh, final):
        o_acc, m_acc, l_acc = acc
        if not fresh:
            m_old = rows(m_acc, start, stride)
            m_new = jnp.maximum(m_old, m)
            a = jnp.exp2(m_old - m_new)
            b = jnp.exp2(m - m_new)
            pv = a * rows(o_acc, start, stride) + b * pv
            l = a * rows(l_acc, start, stride) + b * l
            m = m_new
        if final:
            put(o_acc, start, stride, pv * (1.0 / l))
        else:
            put(o_acc, start, stride, pv)
            put(m_acc, start, stride, m)
            put(l_acc, start, stride, l)

    def group(gi, src, acc, stride, n_blocks, locate, fresh=False, final=False):
        def body(it, carry):
            starts, parts = [], []
            for u in range(BLOCKS_PER_STEP):
                start, prev, first = locate(it * BLOCKS_PER_STEP + u)
                starts.append(start)
                parts.append(scores(gi, src, stride, start, prev, first))
            for start, part in zip(starts, parts):
                accumulate(acc, stride, start, *part, fresh, final)
            return carry

        lax.fori_loop(0, n_blocks // BLOCKS_PER_STEP, body, 0)

    def windowed(blk, blocks_per_class):
        i = blk % blocks_per_class
        start = pl.multiple_of(blk * CHUNK, CHUNK)
        prev = pl.multiple_of((blk - jnp.minimum(i, 1)) * CHUNK, CHUNK)
        return start, prev, jnp.where(i == 0, 1, 0)

    def class16(blk):
        return (blk % s1) * lc4 + blk // s1, None, None

    for src, dst in ((q_ref, q4_ref), (k_ref, k4_ref), (v_ref, v4_ref)):
        for r in range(s1):
            dst[0, r * lc4:(r + 1) * lc4, :] = src[0, pl.ds(r, lc4, stride=s1), :]

    token_order = (q_ref, k_ref, v_ref), (o_ref, m_ref, l_ref)
    class_order = (q4_ref, k4_ref, v4_ref), (o4_ref, m4_ref, l4_ref)
    n_blocks = seq // CHUNK
    group(1, *class_order, 1, n_blocks, lambda blk: windowed(blk, lc4 // CHUNK), fresh=True)
    group(2, *class_order, s1, n_blocks, class16)
    for src, dst in zip(class_order[1], token_order[1]):
        for r in range(s1):
            dst[0, pl.ds(r, lc4, stride=s1), :] = src[0, r * lc4:(r + 1) * lc4, :]
    group(0, *token_order, 1, n_blocks, lambda blk: windowed(blk, n_blocks), final=True)


def _attn_prompt(q, k, v, bias, bsz, seq):
    assert all(seq % (d * CHUNK) == 0 for d in DILATIONS)
    pairs = N_HEADS_A // 2
    view = lambda a: a.reshape(bsz, seq, D_ATTN)
    blk = pl.BlockSpec((1, seq, LANES), lambda b, p: (b, 0, p))
    o = pl.pallas_call(
        functools.partial(_attn_prompt_kernel, seq=seq),
        grid=(bsz, pairs),
        in_specs=[blk, blk, blk,
                  pl.BlockSpec((len(DILATIONS), 1, 2, 2 * CHUNK, 2 * CHUNK), lambda b, p: (0, p, 0, 0, 0))],
        out_specs=blk,
        out_shape=jax.ShapeDtypeStruct((bsz, seq, D_ATTN), F32),
        scratch_shapes=[pltpu.VMEM((1, seq, LANES), F32)] * 8,
        compiler_params=_cparams("parallel", "parallel"),
        name="attn_prompt",
    )(view(q), view(k), view(v), bias)
    return o.reshape(bsz * seq, D_ATTN)


def _attn_sample_kernel(q_ref, kn_ref, vn_ref, ck_ref, cv_ref, bias_ref, o_ref, *, lq, buf):
    nrow = N_HEADS_A * lq
    pad = jnp.zeros((CHUNK - lq, D_ATTN), F32)
    row_head = _div_pow2(lax.broadcasted_iota(jnp.int32, (nrow, D_ATTN), 0), lq)
    col_head = _div_pow2(lax.broadcasted_iota(jnp.int32, (nrow, D_ATTN), 1), HEAD_DIM_A)
    own = row_head == col_head
    qs = jnp.where(own, jnp.concatenate([q_ref[0]] * N_HEADS_A, axis=0), 0.0).astype(BF16)
    kn = jnp.concatenate([kn_ref[0], pad], axis=0).astype(BF16)
    vn = jnp.concatenate([vn_ref[0], pad], axis=0).astype(BF16)
    s_c = _dot(qs, ck_ref[0].astype(BF16)) + bias_ref[:, 0:buf]
    s_n = _dot_nt(qs, kn) + bias_ref[:, buf:buf + CHUNK]
    m = jnp.maximum(jnp.max(s_c, axis=1, keepdims=True), jnp.max(s_n, axis=1, keepdims=True))
    e_c = jnp.exp2(s_c - m)
    e_n = jnp.exp2(s_n - m)
    l = jnp.sum(e_c, axis=1, keepdims=True) + jnp.sum(e_n, axis=1, keepdims=True)
    pv = _dot_nt(e_c.astype(BF16), cv_ref[0].astype(BF16)) + _dot(e_n.astype(BF16), vn)
    pv = jnp.where(own, pv * (1.0 / l), 0.0)
    o = pv[0:lq]
    for h in range(1, N_HEADS_A):
        o = o + pv[h * lq:(h + 1) * lq]
    o_ref[0] = o


def _attn_sample(q, k_new, v_new, cache_kt, cache_vt, bias, bsz, lq):
    buf = cache_kt.shape[2]
    assert lq & (lq - 1) == 0 and lq <= CHUNK
    tok = pl.BlockSpec((1, lq, D_ATTN), lambda b: (b, 0, 0))
    cache = pl.BlockSpec((1, D_ATTN, buf), lambda b: (b, 0, 0))
    per_seq = lambda a: a.reshape(bsz, lq, D_ATTN)
    o = pl.pallas_call(
        functools.partial(_attn_sample_kernel, lq=lq, buf=buf),
        grid=(bsz,),
        in_specs=[tok, tok, tok, cache, cache, pl.BlockSpec(bias.shape, lambda b: (0, 0))],
        out_specs=tok,
        out_shape=jax.ShapeDtypeStruct((bsz, lq, D_ATTN), F32),
        compiler_params=_cparams("parallel"),
        name="attn_sample",
    )(per_seq(q), per_seq(k_new), per_seq(v_new), cache_kt, cache_vt, bias)
    return o.reshape(bsz * lq, D_ATTN)


def _ssd_kernel(*refs, rows, has_state, lookahead):
    (h_ref, hn_ref, z_ref, past_ref) = refs[:4]
    refs = refs[4:]
    if has_state:
        h0_ref = refs[0]
        refs = refs[1:]
    (wc_ref, cw_ref, cb_ref, dtb_ref, a_ref, dsk_ref, nrm_ref, tri_ref, rep_ref, hmask_ref,
     y_ref, hout_ref, tail_ref, xpad_ref, xc_ref, dtc_ref, xnext_ref) = refs
    c = pl.program_id(1)

    pieces = [(j * 1024, 1024) for j in range(D_CONV // 1024)] + [(D_CONV, LANES)]

    def project(src_ref, put, which=None):
        hin = src_ref[0]
        if rows < CHUNK:
            hin = jnp.concatenate([hin.astype(F32), jnp.zeros((CHUNK - rows, D_MODEL), F32)],
                                  axis=0).astype(BF16)
        for c0, width in (pieces if which is None else pieces[which:which + 1]):
            put(c0, width, _dot(hin, wc_ref[:, c0:c0 + width]))

    def put_current(c0, width, val):
        if c0 < D_CONV:
            xpad_ref[SUBLANES:SUBLANES + CHUNK, c0:c0 + width] = val
        else:
            dtc_ref[...] = val

    def put_next(c0, width, val):
        xnext_ref[:, c0:c0 + width] = val

    @pl.when(c == 0)
    def _():
        xpad_ref[0:SUBLANES, :] = past_ref[0]
        if has_state:
            hout_ref[0] = h0_ref[0]
        else:
            hout_ref[0] = jnp.zeros((D_SSM, D_STATE), F32)
        project(h_ref, put_current)

    @pl.when(c > 0)
    def _():
        xpad_ref[0:SUBLANES, :] = xpad_ref[CHUNK:CHUNK + SUBLANES, :]
        xpad_ref[SUBLANES:SUBLANES + CHUNK, :] = xnext_ref[:, 0:D_CONV]
        dtc_ref[...] = xnext_ref[:, D_CONV:D_CONV + LANES]

    dt_raw = dtc_ref[...]

    @pl.when(c == pl.num_programs(1) - 1)
    def _():
        tail_ref[0] = xpad_ref[rows:rows + SUBLANES, :]

    slab = 512
    for j in range(D_CONV // slab):
        cs = slice(j * slab, (j + 1) * slab)
        xp = xpad_ref[:, cs]
        acc = cb_ref[:, cs] + xp[SUBLANES:] * cw_ref[CONV_WIDTH - 1:CONV_WIDTH, cs]
        for shift in range(1, CONV_WIDTH):
            tap = CONV_WIDTH - 1 - shift
            acc = acc + pltpu.roll(xp, shift, axis=0)[SUBLANES:] * cw_ref[tap:tap + 1, cs]
        xc_ref[:, cs] = _silu(acc)

    pre = dt_raw + dtb_ref[...]
    dt = jnp.maximum(pre, 0.0) + jnp.log(1.0 + jnp.exp(-jnp.abs(pre)))
    if rows < CHUNK:
        rid = lax.broadcasted_iota(jnp.int32, (CHUNK, LANES), 0)
        dt = jnp.where(rid < rows, dt, 0.0)
    da = dt * a_ref[...]
    acum = jnp.dot(tri_ref[...], da, preferred_element_type=F32, precision=lax.Precision.HIGHEST)
    acum_t = acum.T
    last = acum[CHUNK - 1:CHUNK, :]
    e_acum = jnp.exp(acum)
    w_coef = dt * jnp.exp(last - acum)

    ri = lax.broadcasted_iota(jnp.int32, (CHUNK, CHUNK), 0)
    ci = lax.broadcasted_iota(jnp.int32, (CHUNK, CHUNK), 1)
    causal = ri >= ci
    chunk_decay = jnp.exp(last)

    coef = jnp.concatenate([dt, e_acum, w_coef], axis=0)
    hi = coef.astype(BF16)
    lo = (coef - hi.astype(F32)).astype(BF16)
    wide = _dot(jnp.concatenate([hi, lo], axis=1), rep_ref[...])
    dt_w, e_w, w_w = wide[0:CHUNK], wide[CHUNK:2 * CHUNK], wide[2 * CHUNK:3 * CHUNK]

    for g in range(SSM_GROUPS):
        if lookahead and g < len(pieces):
            project(hn_ref, put_next, g)
        gs = slice(g * GROUP_WIDTH, (g + 1) * GROUP_WIDTH)
        bg = xc_ref[:, D_SSM + g * D_STATE:D_SSM + (g + 1) * D_STATE].astype(BF16)
        cg = xc_ref[:, D_SSM + D_BC + g * D_STATE:D_SSM + D_BC + (g + 1) * D_STATE].astype(BF16)
        xg = xc_ref[:, gs]
        cb = _dot_nt(cg, bg)
        xdt = (xg * dt_w[:, gs]).astype(BF16)
        m_parts, x_parts = [], []
        for k in range(HEADS_PER_GROUP):
            h = HEADS_PER_GROUP * g + k
            seg = acum[:, h:h + 1] - acum_t[h:h + 1, :]
            decay = jnp.exp(jnp.where(causal, seg, NEG))
            m_parts.append((cb * decay).astype(BF16))
            x_parts.append(xdt * hmask_ref[k])
        y = _dot(jnp.concatenate(m_parts, axis=1), jnp.concatenate(x_parts, axis=0))
        hg = hout_ref[0, gs, :]
        y = y + _dot_nt(cg, hg.astype(BF16)) * e_w[:, gs]
        y = y + dsk_ref[:, gs] * xg
        upd = _dot((xg * w_w[:, gs]).T.astype(BF16), bg)
        for k in range(HEADS_PER_GROUP):
            h = HEADS_PER_GROUP * g + k
            hs = slice(k * SSM_HEAD_DIM, (k + 1) * SSM_HEAD_DIM)
            hout_ref[0, h * SSM_HEAD_DIM:(h + 1) * SSM_HEAD_DIM, :] = chunk_decay[:, h:h + 1] * hg[hs] + upd[hs]
        yz = y[:rows] * _silu(z_ref[0, :, gs])
        ms = jnp.mean(yz * yz, axis=1, keepdims=True)
        y_ref[0, :, gs] = (yz * lax.rsqrt(ms + 1e-5) * nrm_ref[:, gs]).astype(y_ref.dtype)


def _ssd(h, z, past8, h0, wts, bsz, seq):
    rows = CHUNK if seq % CHUNK == 0 else seq
    nc = seq // rows
    tok = lambda n: pl.BlockSpec((1, rows, n), lambda b, c: (b * nc + c, 0, 0))
    full2 = lambda a: pl.BlockSpec(a.shape, lambda b, c: (0,) * a.ndim)
    state = pl.BlockSpec((1, D_SSM, D_STATE), lambda b, c: (b, 0, 0))
    edge = pl.BlockSpec((1, SUBLANES, D_CONV), lambda b, c: (b, 0, 0))
    has_state = h0 is not None
    chunks = lambda a: a.reshape(bsz * nc, rows, a.shape[-1])
    nxt = pl.BlockSpec((1, rows, D_MODEL), lambda b, c: (b * nc + jnp.minimum(c + 1, nc - 1), 0, 0))
    ins = [chunks(h), chunks(h), chunks(z), past8] + ([h0] if has_state else []) + list(wts)
    in_specs = ([tok(D_MODEL), nxt, tok(D_SSM), edge] + ([state] if has_state else [])
                + [full2(w) for w in wts])
    y, h_new, tail = pl.pallas_call(
        functools.partial(_ssd_kernel, rows=rows, has_state=has_state, lookahead=nc > 1),
        grid=(bsz, nc),
        in_specs=in_specs,
        out_specs=[tok(D_SSM), state, edge],
        out_shape=[jax.ShapeDtypeStruct((bsz * nc, rows, D_SSM), BF16),
                   jax.ShapeDtypeStruct((bsz, D_SSM, D_STATE), F32),
                   jax.ShapeDtypeStruct((bsz, SUBLANES, D_CONV), F32)],
        scratch_shapes=[pltpu.VMEM((SUBLANES + CHUNK, D_CONV), F32),
                        pltpu.VMEM((CHUNK, D_CONV), F32),
                        pltpu.VMEM((CHUNK, LANES), F32),
                        pltpu.VMEM((CHUNK, D_CONV + LANES), F32)],
        compiler_params=_cparams("parallel", "arbitrary"),
        name="ssd",
    )(*ins)
    return y.reshape(bsz * seq, D_SSM), h_new, tail


def _merge_kernel(o_ref, g_ref, ys_ref, ga_ref, gb_ref, x_ref, wa_ref, wb_ref, wo_ref, fn_ref, y_ref):
    ya = (o_ref[...] * _silu(g_ref[...])).astype(BF16)
    merged = (_sigmoid(ga_ref[...]) * _dot(ya, wa_ref[...])
              + _sigmoid(gb_ref[...]) * _dot(ys_ref[...], wb_ref[...]))
    out = x_ref[...] + _dot(merged.astype(BF16), wo_ref[...])
    ms = jnp.mean(out * out, axis=-1, keepdims=True)
    y_ref[...] = out * lax.rsqrt(ms + 1e-6) * fn_ref[...]


def _merge(o, g_attn, ys, gate_a, gate_b, x, wa, wb, wo, fnorm):
    t = x.shape[0]
    tm = _row_tile(t)
    row = lambda n: pl.BlockSpec((tm, n), lambda i: (i, 0))
    full = lambda a: pl.BlockSpec(a.shape, lambda i: (0, 0))
    return pl.pallas_call(
        _merge_kernel,
        grid=(t // tm,),
        in_specs=[row(D_ATTN), row(D_ATTN), row(D_SSM), row(D_MODEL), row(D_MODEL), row(D_MODEL),
                  full(wa), full(wb), full(wo), full(fnorm)],
        out_specs=row(D_MODEL),
        out_shape=jax.ShapeDtypeStruct((t, D_MODEL), F32),
        compiler_params=_cparams("parallel"),
        name="merge",
    )(o, g_attn, ys, gate_a, gate_b, x, wa, wb, wo, fnorm)


def _prep_weights(norm_g, w_in, conv_w, conv_b, dt_bias, a_log, d_skip, ssm_norm,
                  w_branch_a, w_branch_b, w_out, final_norm):
    sizes = (D_ATTN, D_ATTN, D_ATTN, D_ATTN, D_SSM, D_CONV, N_SSM_HEADS, D_MODEL, D_MODEL)
    pts = np.cumsum((0,) + sizes)
    seg = lambda i: w_in[:, pts[i]:pts[i + 1]]
    pad_heads = lambda v: jnp.pad(v.astype(F32), (0, LANES - N_SSM_HEADS)).reshape(1, LANES)
    w_a = w_in[:, 0:pts[4]].astype(BF16)
    w_b = jnp.concatenate([seg(4), seg(7), seg(8)], axis=1).astype(BF16)
    w_c = jnp.concatenate([seg(5), jnp.pad(seg(6), ((0, 0), (0, LANES - N_SSM_HEADS)))], axis=1).astype(BF16)
    tri = jnp.asarray(np.tril(np.ones((CHUNK, CHUNK), np.float32)))
    head_of = np.arange(D_SSM) // SSM_HEAD_DIM
    rep = (np.arange(LANES)[:, None] == head_of[None, :]).astype(np.float32)
    rep2 = jnp.asarray(np.concatenate([rep, rep], axis=0)).astype(BF16)
    hmask = jnp.asarray(np.broadcast_to(
        (np.arange(HEADS_PER_GROUP)[:, None, None] == head_of[None, None, :GROUP_WIDTH]),
        (HEADS_PER_GROUP, CHUNK, GROUP_WIDTH)).astype(np.float32)).astype(BF16)
    ssd_w = (w_c, conv_w.astype(F32), conv_b.astype(F32).reshape(1, D_CONV), pad_heads(dt_bias),
             pad_heads(-jnp.exp(a_log.astype(F32))),
             jnp.repeat(d_skip.astype(F32), SSM_HEAD_DIM).reshape(1, D_SSM),
             ssm_norm.astype(F32).reshape(1, D_SSM), tri, rep2, hmask)
    return dict(norm_g=norm_g.astype(F32).reshape(1, D_MODEL), w_a=w_a, w_b=w_b, ssd_w=ssd_w,
                wa=w_branch_a.astype(BF16), wb=w_branch_b.astype(BF16),
                wo=w_out.astype(BF16), fnorm=final_norm.astype(F32).reshape(1, D_MODEL))


def _layer(x, cache_k, cache_v, conv_past, ssm_past, wts, rel_bias):
    bsz, seq, _ = x.shape
    t = bsz * seq
    x2 = x.reshape(t, D_MODEL)
    h, q, k, v, g_attn, *kv_t = _proj_a(x2, wts["norm_g"], wts["w_a"], bsz, seq)
    z, gate_a, gate_b = _proj(h, wts["w_b"], (D_SSM, D_MODEL, D_MODEL), "proj_b")

    if cache_k is None:
        o = _attn_prompt(q, k, v, _window_bias(rel_bias), bsz, seq)
        past8 = jnp.zeros((bsz, SUBLANES, D_CONV), F32)
    else:
        buf = cache_k.shape[1]
        feature_major = lambda a: jnp.transpose(a, (0, 2, 3, 1)).reshape(bsz, D_ATTN, buf)
        o = _attn_sample(q, k, v, feature_major(cache_k), feature_major(cache_v),
                         _sample_bias(rel_bias, seq, buf), bsz, seq)
        past8 = jnp.pad(conv_past.astype(F32), ((0, 0), (SUBLANES - (CONV_WIDTH - 1), 0), (0, 0)))

    h0 = None if ssm_past is None else ssm_past.astype(F32).reshape(bsz, D_SSM, D_STATE)
    ys, ssm_new, tail = _ssd(h, z, past8, h0, wts["ssd_w"], bsz, seq)
    y = _merge(o, g_attn, ys, gate_a, gate_b, x2, wts["wa"], wts["wb"], wts["wo"], wts["fnorm"])
    conv_new = tail[:, SUBLANES - (CONV_WIDTH - 1):]
    if kv_t:
        per_head = lambda a: jnp.transpose(a.reshape(bsz, N_HEADS_A, HEAD_DIM_A, seq), (0, 3, 1, 2))
        k_out, v_out = per_head(kv_t[0]), per_head(kv_t[1])
    else:
        k_out = k.reshape(bsz, seq, N_HEADS_A, HEAD_DIM_A)
        v_out = v.reshape(bsz, seq, N_HEADS_A, HEAD_DIM_A)
    return (y.reshape(bsz, seq, D_MODEL), k_out, v_out, conv_new,
            ssm_new.reshape(bsz, N_SSM_HEADS, SSM_HEAD_DIM, D_STATE))


def kernel(x_prompt, x_sample, cache_k, cache_v, state_conv, state_ssm, norm_g, w_in, conv_w, conv_b,
           dt_bias, a_log, d_skip, ssm_norm, w_branch_a, w_branch_b, w_out, rel_bias, final_norm):
    assert w_in.shape[0] == 1, "single layer"
    wts = _prep_weights(norm_g[0], w_in[0], conv_w[0], conv_b[0], dt_bias[0], a_log[0], d_skip[0],
                        ssm_norm[0], w_branch_a[0], w_branch_b[0], w_out[0], final_norm)
    keep = min(WINDOW_MAX, x_prompt.shape[1])
    yp, kp, vp, cp, sp = _layer(x_prompt, None, None, None, None, wts, rel_bias)
    ys, ks, vs, cs, ss = _layer(x_sample, cache_k[0], cache_v[0], state_conv[0], state_ssm[0], wts, rel_bias)
    return (yp, ys, kp[:, -keep:][None], vp[:, -keep:][None], cp[None], sp[None],
            ks[None], vs[None], cs[None], ss[None])
```

```python
import functools
import math

import numpy as np
import jax
import jax.numpy as jnp
from jax import lax
from jax.experimental import pallas as pl
from jax.experimental.pallas import tpu as pltpu

F32 = jnp.float32
BF16 = jnp.bfloat16

D_MODEL = 1024
N_HEADS_A = 12
HEAD_DIM_A = 64
D_ATTN = N_HEADS_A * HEAD_DIM_A
DILATIONS = (1, 4, 16)
WINDOW_KEYS = 128
WINDOW_MAX = 2048
N_BUCKETS = 32
MAX_DISTANCE = WINDOW_MAX
D_SSM = 2048
SSM_HEAD_DIM = 64
N_SSM_HEADS = D_SSM // SSM_HEAD_DIM
SSM_GROUPS = 8
HEADS_PER_GROUP = N_SSM_HEADS // SSM_GROUPS
GROUP_WIDTH = D_SSM // SSM_GROUPS
D_STATE = 128
CONV_WIDTH = 4
D_BC = SSM_GROUPS * D_STATE
D_CONV = D_SSM + 2 * D_BC
CHUNK = 128
LANES = 128
SUBLANES = 8
NEG = -1e30
BLOCKS_PER_STEP = 8
LOG2E = math.log2(math.e)
VMEM_LIMIT = 48 * 1024 * 1024


def _cparams(*sem):
    return pltpu.CompilerParams(dimension_semantics=sem, vmem_limit_bytes=VMEM_LIMIT)


def _dot(a, b):
    return jnp.dot(a, b, preferred_element_type=F32)


def _dot_nt(a, b):
    return lax.dot_general(a, b, (((1,), (1,)), ((), ())), preferred_element_type=F32)


def _silu(x):
    return x * (1.0 / (1.0 + jnp.exp(-x)))


def _sigmoid(x):
    return 1.0 / (1.0 + jnp.exp(-x))


def _div_pow2(x, n):
    assert n & (n - 1) == 0
    return jnp.right_shift(x, int(math.log2(n)))


def _proj_a_kernel(x_ref, g_ref, w_ref, h_ref, q_ref, k_ref, v_ref, ga_ref, *t_refs):
    xf = x_ref[...]
    ms = jnp.mean(xf * xf, axis=-1, keepdims=True)
    h = (xf * lax.rsqrt(ms + 1e-6) * g_ref[...]).astype(BF16)
    h_ref[...] = h
    q_ref[...] = _dot(h, w_ref[:, 0:D_ATTN]) * (LOG2E / math.sqrt(HEAD_DIM_A))
    k = _dot(h, w_ref[:, D_ATTN:2 * D_ATTN])
    k_ref[...] = k
    v = _dot(h, w_ref[:, 2 * D_ATTN:3 * D_ATTN])
    v_ref[...] = v
    ga_ref[...] = _dot(h, w_ref[:, 3 * D_ATTN:4 * D_ATTN])
    if t_refs:
        kt_ref, vt_ref = t_refs
        kt_ref[0] = k.T
        vt_ref[0] = v.T


def _proj_kernel(h_ref, w_ref, *out_refs, widths, silu_first):
    h = h_ref[...]
    c0 = 0
    for o_ref, wd in zip(out_refs, widths):
        val = _dot(h, w_ref[:, c0:c0 + wd])
        o_ref[...] = _silu(val) if silu_first and c0 == 0 else val
        c0 += wd


def _row_tile(t):
    return 512 if t % 512 == 0 else t


def _proj_a(x, norm_g, w_a, bsz, seq):
    t = x.shape[0]
    tm = _row_tile(t)
    row = lambda n: pl.BlockSpec((tm, n), lambda i: (i, 0))
    full = lambda a: pl.BlockSpec(a.shape, lambda i: (0, 0))
    outs = [jax.ShapeDtypeStruct((t, D_MODEL), BF16)] + [jax.ShapeDtypeStruct((t, D_ATTN), F32)] * 4
    out_specs = [row(s.shape[1]) for s in outs]
    if seq % tm == 0:
        per_seq = seq // tm
        outs += [jax.ShapeDtypeStruct((bsz, D_ATTN, seq), F32)] * 2
        out_specs += [pl.BlockSpec((1, D_ATTN, tm), lambda i: (i // per_seq, 0, i % per_seq))] * 2
    return pl.pallas_call(
        _proj_a_kernel,
        grid=(t // tm,),
        in_specs=[row(D_MODEL), full(norm_g), full(w_a)],
        out_specs=out_specs,
        out_shape=outs,
        compiler_params=_cparams("parallel"),
        name="proj_a",
    )(x, norm_g, w_a)


def _proj(h, w, widths, name, silu_first=False):
    t = h.shape[0]
    tm = _row_tile(t)
    row = lambda n: pl.BlockSpec((tm, n), lambda i: (i, 0))
    outs = [jax.ShapeDtypeStruct((t, wd), F32) for wd in widths]
    return pl.pallas_call(
        functools.partial(_proj_kernel, widths=widths, silu_first=silu_first),
        grid=(t // tm,),
        in_specs=[row(D_MODEL), pl.BlockSpec(w.shape, lambda i: (0, 0))],
        out_specs=[row(wd) for wd in widths],
        out_shape=outs,
        compiler_params=_cparams("parallel"),
        name=name,
    )(h, w)


def _t5_bucket(dist):
    max_exact = N_BUCKETS // 2
    d = np.maximum(dist, 1).astype(np.float32)
    large = max_exact + (np.log(d / max_exact) / math.log(MAX_DISTANCE / max_exact)
                         * (N_BUCKETS - max_exact)).astype(np.int32)
    large = np.minimum(large, N_BUCKETS - 1)
    return np.where(dist < max_exact, dist, large).astype(np.int32)


def _bias_by_distance(rel_bias, dist, valid, extra=None):
    onehot = (_t5_bucket(np.maximum(dist, 0))[:, None] == np.arange(N_BUCKETS)[None, :]) & valid[:, None]
    b = jnp.dot(jnp.asarray(onehot.astype(np.float32)), rel_bias.astype(F32), precision=lax.Precision.HIGHEST)
    if extra is not None:
        b = b + jnp.asarray(extra.astype(np.float32))[:, None]
    return jnp.where(jnp.asarray(valid)[:, None], b * LOG2E, NEG)


def _window_bias(rel_bias):
    u = np.arange(2 * CHUNK)
    has_prev = jnp.asarray(np.arange(2 * CHUNK) >= CHUNK)
    out = []
    for d in DILATIONS:
        g = _bias_by_distance(rel_bias, (WINDOW_KEYS - u) * d, u <= WINDOW_KEYS).T
        reps = jnp.tile(g, (1, CHUNK * 2))[:, :CHUNK * (4 * CHUNK - 1)]
        toep = reps.reshape(N_HEADS_A, CHUNK, 4 * CHUNK - 1)[:, :, :2 * CHUNK]
        toep = toep.reshape(N_HEADS_A // 2, 2 * CHUNK, 2 * CHUNK)
        out.append(jnp.stack([toep, jnp.where(has_prev, toep, NEG)], axis=1))
    return jnp.stack(out)


def _sample_bias(rel_bias, lq, buf):
    ncol = buf + CHUNK
    dist = np.arange(buf + lq - 1, -(CHUNK + lq) - 1, -1)
    mult = np.zeros(dist.shape, np.int32)
    for d in DILATIONS:
        mult += ((dist >= 0) & (dist % d == 0) & (dist <= WINDOW_KEYS * d)).astype(np.int32)
    fr = _bias_by_distance(rel_bias, dist, mult > 0, np.log(np.maximum(mult, 1))).T
    rows = [fr[:, lq - 1 - i:lq - 1 - i + ncol] for i in range(lq)]
    return jnp.stack(rows, axis=1).reshape(N_HEADS_A * lq, ncol)


def _attn_prompt_kernel(q_ref, k_ref, v_ref, bias_ref, o_ref, m_ref, l_ref,
                        q4_ref, k4_ref, v4_ref, o4_ref, m4_ref, l4_ref, *, seq):
    s1 = DILATIONS[1]
    assert DILATIONS == (1, s1, s1 * s1) and seq // DILATIONS[2] == CHUNK
    lc4 = seq // s1
    lane = lax.broadcasted_iota(jnp.int32, (CHUNK, LANES), 1)
    first_half = lane < HEAD_DIM_A

    def rows(ref, start, stride):
        if stride == 1:
            return ref[0, pl.ds(start, CHUNK), :]
        return ref[0, pl.ds(start, CHUNK, stride=stride), :]

    def put(ref, start, stride, val):
        if stride == 1:
            ref[0, pl.ds(start, CHUNK), :] = val
        else:
            ref[0, pl.ds(start, CHUNK, stride=stride), :] = val

    def pair_tile(x):
        return jnp.where(first_half, x[:CHUNK], x[CHUNK:])

    def scores(gi, src, stride, start, prev_start, first):
        qr, kr, vr = src
        qp = rows(qr, start, stride)
        qs = jnp.concatenate([jnp.where(first_half, qp, 0.0),
                              jnp.where(first_half, 0.0, qp)], axis=0).astype(BF16)
        kw = rows(kr, start, stride).astype(BF16)
        vw = rows(vr, start, stride).astype(BF16)
        if prev_start is None:
            s = _dot_nt(qs, kw) + bias_ref[gi, 0, 0, :, CHUNK:]
        else:
            kw = jnp.concatenate([rows(kr, prev_start, stride).astype(BF16), kw], axis=0)
            vw = jnp.concatenate([rows(vr, prev_start, stride).astype(BF16), vw], axis=0)
            s = _dot_nt(qs, kw) + bias_ref[gi, 0, first]
        m = jnp.max(s, axis=1, keepdims=True)
        e = jnp.exp2(s - m)
        l = pair_tile(jnp.sum(e, axis=1, keepdims=True))
        pv = pair_tile(_dot(e.astype(BF16), vw))
        return pv, pair_tile(m), l

    def accumulate(acc, stride, start, pv, m, l, fresh, final):
        o_acc, m_acc, l_acc = acc
        if not fresh:
            m_old = rows(m_acc, start, stride)
            m_new = jnp.maximum(m_old, m)
            a = jnp.exp2(m_old - m_new)
            b = jnp.exp2(m - m_new)
            pv = a * rows(o_acc, start, stride) + b * pv
            l = a * rows(l_acc, start, stride) + b * l
            m = m_new
        if final:
            put(o_acc, start, stride, pv * (1.0 / l))
        else:
            put(o_acc, start, stride, pv)
            put(m_acc, start, stride, m)
            put(l_acc, start, stride, l)

    def group(gi, src, acc, stride, n_blocks, locate, fresh=False, final=False):
        def body(it, carry):
            starts, parts = [], []
            for u in range(BLOCKS_PER_STEP):
                start, prev, first = locate(it * BLOCKS_PER_STEP + u)
                starts.append(start)
                parts.append(scores(gi, src, stride, start, prev, first))
            for start, part in zip(starts, parts):
                accumulate(acc, stride, start, *part, fresh, final)
            return carry

        lax.fori_loop(0, n_blocks // BLOCKS_PER_STEP, body, 0)

    def windowed(blk, blocks_per_class):
        i = blk % blocks_per_class
        start = pl.multiple_of(blk * CHUNK, CHUNK)
        prev = pl.multiple_of((blk - jnp.minimum(i, 1)) * CHUNK, CHUNK)
        return start, prev, jnp.where(i == 0, 1, 0)

    def class16(blk):
        return (blk % s1) * lc4 + blk // s1, None, None

    for src, dst in ((q_ref, q4_ref), (k_ref, k4_ref), (v_ref, v4_ref)):
        for r in range(s1):
            dst[0, r * lc4:(r + 1) * lc4, :] = src[0, pl.ds(r, lc4, stride=s1), :]

    token_order = (q_ref, k_ref, v_ref), (o_ref, m_ref, l_ref)
    class_order = (q4_ref, k4_ref, v4_ref), (o4_ref, m4_ref, l4_ref)
    n_blocks = seq // CHUNK
    group(1, *class_order, 1, n_blocks, lambda blk: windowed(blk, lc4 // CHUNK), fresh=True)
    group(2, *class_order, s1, n_blocks, class16)
    for src, dst in zip(class_order[1], token_order[1]):
        for r in range(s1):
            dst[0, pl.ds(r, lc4, stride=s1), :] = src[0, r * lc4:(r + 1) * lc4, :]
    group(0, *token_order, 1, n_blocks, lambda blk: windowed(blk, n_blocks), final=True)


def _attn_prompt(q, k, v, bias, bsz, seq):
    assert all(seq % (d * CHUNK) == 0 for d in DILATIONS)
    pairs = N_HEADS_A // 2
    view = lambda a: a.reshape(bsz, seq, D_ATTN)
    blk = pl.BlockSpec((1, seq, LANES), lambda b, p: (b, 0, p))
    o = pl.pallas_call(
        functools.partial(_attn_prompt_kernel, seq=seq),
        grid=(bsz, pairs),
        in_specs=[blk, blk, blk,
                  pl.BlockSpec((len(DILATIONS), 1, 2, 2 * CHUNK, 2 * CHUNK), lambda b, p: (0, p, 0, 0, 0))],
        out_specs=blk,
        out_shape=jax.ShapeDtypeStruct((bsz, seq, D_ATTN), F32),
        scratch_shapes=[pltpu.VMEM((1, seq, LANES), F32)] * 8,
        compiler_params=_cparams("parallel", "parallel"),
        name="attn_prompt",
    )(view(q), view(k), view(v), bias)
    return o.reshape(bsz * seq, D_ATTN)


def _attn_sample_kernel(q_ref, kn_ref, vn_ref, ck_ref, cv_ref, bias_ref, o_ref, *, lq, buf):
    nrow = N_HEADS_A * lq
    pad = jnp.zeros((CHUNK - lq, D_ATTN), F32)
    row_head = _div_pow2(lax.broadcasted_iota(jnp.int32, (nrow, D_ATTN), 0), lq)
    col_head = _div_pow2(lax.broadcasted_iota(jnp.int32, (nrow, D_ATTN), 1), HEAD_DIM_A)
    own = row_head == col_head
    qs = jnp.where(own, jnp.concatenate([q_ref[0]] * N_HEADS_A, axis=0), 0.0).astype(BF16)
    kn = jnp.concatenate([kn_ref[0], pad], axis=0).astype(BF16)
    vn = jnp.concatenate([vn_ref[0], pad], axis=0).astype(BF16)
    s_c = _dot(qs, ck_ref[0].astype(BF16)) + bias_ref[:, 0:buf]
    s_n = _dot_nt(qs, kn) + bias_ref[:, buf:buf + CHUNK]
    m = jnp.maximum(jnp.max(s_c, axis=1, keepdims=True), jnp.max(s_n, axis=1, keepdims=True))
    e_c = jnp.exp2(s_c - m)
    e_n = jnp.exp2(s_n - m)
    l = jnp.sum(e_c, axis=1, keepdims=True) + jnp.sum(e_n, axis=1, keepdims=True)
    pv = _dot_nt(e_c.astype(BF16), cv_ref[0].astype(BF16)) + _dot(e_n.astype(BF16), vn)
    pv = jnp.where(own, pv * (1.0 / l), 0.0)
    o = pv[0:lq]
    for h in range(1, N_HEADS_A):
        o = o + pv[h * lq:(h + 1) * lq]
    o_ref[0] = o


def _attn_sample(q, k_new, v_new, cache_kt, cache_vt, bias, bsz, lq):
    buf = cache_kt.shape[2]
    assert lq & (lq - 1) == 0 and lq <= CHUNK
    tok = pl.BlockSpec((1, lq, D_ATTN), lambda b: (b, 0, 0))
    cache = pl.BlockSpec((1, D_ATTN, buf), lambda b: (b, 0, 0))
    per_seq = lambda a: a.reshape(bsz, lq, D_ATTN)
    o = pl.pallas_call(
        functools.partial(_attn_sample_kernel, lq=lq, buf=buf),
        grid=(bsz,),
        in_specs=[tok, tok, tok, cache, cache, pl.BlockSpec(bias.shape, lambda b: (0, 0))],
        out_specs=tok,
        out_shape=jax.ShapeDtypeStruct((bsz, lq, D_ATTN), F32),
        compiler_params=_cparams("parallel"),
        name="attn_sample",
    )(per_seq(q), per_seq(k_new), per_seq(v_new), cache_kt, cache_vt, bias)
    return o.reshape(bsz * lq, D_ATTN)


def _conv_silu(ext, cw_ref, cb_ref, cs):
    acc = cb_ref[:, cs] + ext[SUBLANES:] * cw_ref[CONV_WIDTH - 1:CONV_WIDTH, cs]
    for shift in range(1, CONV_WIDTH):
        tap = CONV_WIDTH - 1 - shift
        acc = acc + pltpu.roll(ext, shift, axis=0)[SUBLANES:] * cw_ref[tap:tap + 1, cs]
    return _silu(acc)


def _proj_conv_kernel(h_ref, w_ref, cw_ref, cb_ref, past_ref, xs_ref, bm_ref, cm_ref, dt_ref, tail_ref,
                      halo_ref, *, tiles_per_seq):
    i = pl.program_id(0)
    tm = h_ref.shape[0]

    @pl.when(i % tiles_per_seq == 0)
    def _():
        halo_ref[...] = past_ref[0]

    h = h_ref[...]
    slab = 512
    for j in range(D_CONV // slab):
        cs = slice(j * slab, (j + 1) * slab)
        pre = _dot(h, w_ref[:, cs])
        out = _conv_silu(jnp.concatenate([halo_ref[:, cs], pre], axis=0), cw_ref, cb_ref, cs)
        halo_ref[:, cs] = pre[tm - SUBLANES:]
        if cs.stop <= D_SSM:
            xs_ref[:, cs] = out
        elif cs.stop <= D_SSM + D_BC:
            bm_ref[:, cs.start - D_SSM:cs.stop - D_SSM] = out.astype(BF16)
        else:
            cm_ref[:, cs.start - D_SSM - D_BC:cs.stop - D_SSM - D_BC] = out.astype(BF16)
    dt_ref[...] = _dot(h, w_ref[:, D_CONV:D_CONV + LANES])

    @pl.when(i % tiles_per_seq == tiles_per_seq - 1)
    def _():
        tail_ref[0] = halo_ref[...]


def _proj_conv(h, w_c, conv_w, conv_b, past8, bsz, seq):
    t = h.shape[0]
    tm = _row_tile(t)
    assert seq % tm == 0
    per_seq = seq // tm
    row = lambda n: pl.BlockSpec((tm, n), lambda i: (i, 0))
    full = lambda a: pl.BlockSpec(a.shape, lambda i: (0, 0))
    edge = pl.BlockSpec((1, SUBLANES, D_CONV), lambda i: (i // per_seq, 0, 0))
    return pl.pallas_call(
        functools.partial(_proj_conv_kernel, tiles_per_seq=per_seq),
        grid=(t // tm,),
        in_specs=[row(D_MODEL), full(w_c), full(conv_w), full(conv_b), edge],
        out_specs=[row(D_SSM), row(D_BC), row(D_BC), row(LANES), edge],
        out_shape=[jax.ShapeDtypeStruct((t, D_SSM), F32), jax.ShapeDtypeStruct((t, D_BC), BF16),
                   jax.ShapeDtypeStruct((t, D_BC), BF16), jax.ShapeDtypeStruct((t, LANES), F32),
                   jax.ShapeDtypeStruct((bsz, SUBLANES, D_CONV), F32)],
        scratch_shapes=[pltpu.VMEM((SUBLANES, D_CONV), F32)],
        compiler_params=_cparams("arbitrary"),
        name="proj_conv",
    )(h, w_c, conv_w, conv_b, past8)


def _ssd_kernel(*refs, rows, has_state, conv_inside):
    (xs_ref, bm_ref, cm_ref, dt_ref, z_ref) = refs[:5]
    refs = refs[5:]
    if conv_inside:
        past_ref = refs[0]
        refs = refs[1:]
    if has_state:
        h0_ref = refs[0]
        refs = refs[1:]
    if conv_inside:
        cw_ref, cb_ref = refs[:2]
        refs = refs[2:]
    (dtb_ref, a_ref, dsk_ref, nrm_ref, tri_ref, rep_ref, hmask_ref, y_ref, hout_ref) = refs[:9]
    c = pl.program_id(1)

    @pl.when(c == 0)
    def _():
        if has_state:
            hout_ref[0] = h0_ref[0]
        else:
            hout_ref[0] = jnp.zeros((D_SSM, D_STATE), F32)

    if conv_inside:
        tail_ref, xpad_ref, xc_ref = refs[9:]

        @pl.when(c == 0)
        def _():
            xpad_ref[0:SUBLANES, :] = past_ref[0]
            if rows < CHUNK:
                xpad_ref[SUBLANES + rows:SUBLANES + CHUNK, :] = jnp.zeros((CHUNK - rows, D_CONV), F32)

        @pl.when(c > 0)
        def _():
            xpad_ref[0:SUBLANES, :] = xpad_ref[CHUNK:CHUNK + SUBLANES, :]

        xpad_ref[SUBLANES:SUBLANES + rows, 0:D_SSM] = xs_ref[0]
        xpad_ref[SUBLANES:SUBLANES + rows, D_SSM:D_SSM + D_BC] = bm_ref[0]
        xpad_ref[SUBLANES:SUBLANES + rows, D_SSM + D_BC:D_CONV] = cm_ref[0]

        @pl.when(c == pl.num_programs(1) - 1)
        def _():
            tail_ref[0] = xpad_ref[rows:rows + SUBLANES, :]

        slab = 512
        for j in range(D_CONV // slab):
            cs = slice(j * slab, (j + 1) * slab)
            xc_ref[:, cs] = _conv_silu(xpad_ref[:, cs], cw_ref, cb_ref, cs)
        x_of = lambda gs: xc_ref[:, gs]
        b_of = lambda g: xc_ref[:, D_SSM + g * D_STATE:D_SSM + (g + 1) * D_STATE].astype(BF16)
        c_of = lambda g: xc_ref[:, D_SSM + D_BC + g * D_STATE:D_SSM + D_BC + (g + 1) * D_STATE].astype(BF16)
    else:
        assert rows == CHUNK
        x_of = lambda gs: xs_ref[0, :, gs]
        b_of = lambda g: bm_ref[0, :, g * D_STATE:(g + 1) * D_STATE]
        c_of = lambda g: cm_ref[0, :, g * D_STATE:(g + 1) * D_STATE]

    dt_raw = dt_ref[0]
    if rows < CHUNK:
        dt_raw = jnp.concatenate([dt_raw, jnp.zeros((CHUNK - rows, LANES), F32)], axis=0)
    pre = dt_raw + dtb_ref[...]
    dt = jnp.maximum(pre, 0.0) + jnp.log(1.0 + jnp.exp(-jnp.abs(pre)))
    if rows < CHUNK:
        rid = lax.broadcasted_iota(jnp.int32, (CHUNK, LANES), 0)
        dt = jnp.where(rid < rows, dt, 0.0)
    da = dt * a_ref[...]
    acum = jnp.dot(tri_ref[...], da, preferred_element_type=F32, precision=lax.Precision.HIGHEST)
    acum_t = acum.T
    last = acum[CHUNK - 1:CHUNK, :]
    e_acum = jnp.exp(acum)
    w_coef = dt * jnp.exp(last - acum)

    ri = lax.broadcasted_iota(jnp.int32, (CHUNK, CHUNK), 0)
    ci = lax.broadcasted_iota(jnp.int32, (CHUNK, CHUNK), 1)
    causal = ri >= ci
    chunk_decay = jnp.exp(last)

    coef = jnp.concatenate([dt, e_acum, w_coef], axis=0)
    hi = coef.astype(BF16)
    lo = (coef - hi.astype(F32)).astype(BF16)
    wide = _dot(jnp.concatenate([hi, lo], axis=1), rep_ref[...])
    dt_w, e_w, w_w = wide[0:CHUNK], wide[CHUNK:2 * CHUNK], wide[2 * CHUNK:3 * CHUNK]

    for g in range(SSM_GROUPS):
        gs = slice(g * GROUP_WIDTH, (g + 1) * GROUP_WIDTH)
        bg = b_of(g)
        cg = c_of(g)
        xg = x_of(gs)
        cb = _dot_nt(cg, bg)
        xdt = (xg * dt_w[:, gs]).astype(BF16)
        m_parts, x_parts = [], []
        for k in range(HEADS_PER_GROUP):
            h = HEADS_PER_GROUP * g + k
            seg = acum[:, h:h + 1] - acum_t[h:h + 1, :]
            decay = jnp.exp(jnp.where(causal, seg, NEG))
            m_parts.append((cb * decay).astype(BF16))
            x_parts.append(xdt * hmask_ref[k])
        y = _dot(jnp.concatenate(m_parts, axis=1), jnp.concatenate(x_parts, axis=0))
        hg = hout_ref[0, gs, :]
        y = y + _dot_nt(cg, hg.astype(BF16)) * e_w[:, gs]
        y = y + dsk_ref[:, gs] * xg
        upd = _dot((xg * w_w[:, gs]).T.astype(BF16), bg)
        for k in range(HEADS_PER_GROUP):
            h = HEADS_PER_GROUP * g + k
            hs = slice(k * SSM_HEAD_DIM, (k + 1) * SSM_HEAD_DIM)
            hout_ref[0, h * SSM_HEAD_DIM:(h + 1) * SSM_HEAD_DIM, :] = chunk_decay[:, h:h + 1] * hg[hs] + upd[hs]
        yz = y[:rows] * z_ref[0, :, gs]
        ms = jnp.mean(yz * yz, axis=1, keepdims=True)
        y_ref[0, :, gs] = (yz * lax.rsqrt(ms + 1e-5) * nrm_ref[:, gs]).astype(y_ref.dtype)


def _ssd(xs, bm, cm, dt, zs, past8, h0, conv_wts, wts, bsz, seq):
    rows = CHUNK if seq % CHUNK == 0 else seq
    nc = seq // rows
    tok = lambda n: pl.BlockSpec((1, rows, n), lambda b, c: (b * nc + c, 0, 0))
    full2 = lambda a: pl.BlockSpec(a.shape, lambda b, c: (0,) * a.ndim)
    state = pl.BlockSpec((1, D_SSM, D_STATE), lambda b, c: (b, 0, 0))
    edge = pl.BlockSpec((1, SUBLANES, D_CONV), lambda b, c: (b, 0, 0))
    has_state = h0 is not None
    conv_inside = past8 is not None
    chunks = lambda a: a.reshape(bsz * nc, rows, a.shape[-1])
    ins = ([chunks(a) for a in (xs, bm, cm, dt, zs)] + ([past8] if conv_inside else [])
           + ([h0] if has_state else []) + (list(conv_wts) if conv_inside else []) + list(wts))
    in_specs = ([tok(D_SSM), tok(D_BC), tok(D_BC), tok(LANES), tok(D_SSM)] + ([edge] if conv_inside else [])
                + ([state] if has_state else [])
                + [full2(w) for w in (list(conv_wts) if conv_inside else []) + list(wts)])
    out_specs = [tok(D_SSM), state]
    out_shape = [jax.ShapeDtypeStruct((bsz * nc, rows, D_SSM), BF16),
                 jax.ShapeDtypeStruct((bsz, D_SSM, D_STATE), F32)]
    scratch = []
    if conv_inside:
        out_specs.append(edge)
        out_shape.append(jax.ShapeDtypeStruct((bsz, SUBLANES, D_CONV), F32))
        scratch = [pltpu.VMEM((SUBLANES + CHUNK, D_CONV), F32), pltpu.VMEM((CHUNK, D_CONV), F32)]
    y, h_new, *tail = pl.pallas_call(
        functools.partial(_ssd_kernel, rows=rows, has_state=has_state, conv_inside=conv_inside),
        grid=(bsz, nc),
        in_specs=in_specs,
        out_specs=out_specs,
        out_shape=out_shape,
        scratch_shapes=scratch,
        compiler_params=_cparams("parallel", "arbitrary"),
        name="ssd",
    )(*ins)
    return (y.reshape(bsz * seq, D_SSM), h_new) + tuple(tail)


def _merge_kernel(o_ref, g_ref, ys_ref, ga_ref, gb_ref, x_ref, wa_ref, wb_ref, wo_ref, fn_ref, y_ref):
    ya = (o_ref[...] * _silu(g_ref[...])).astype(BF16)
    merged = (_sigmoid(ga_ref[...]) * _dot(ya, wa_ref[...])
              + _sigmoid(gb_ref[...]) * _dot(ys_ref[...], wb_ref[...]))
    out = x_ref[...] + _dot(merged.astype(BF16), wo_ref[...])
    ms = jnp.mean(out * out, axis=-1, keepdims=True)
    y_ref[...] = out * lax.rsqrt(ms + 1e-6) * fn_ref[...]


def _merge(o, g_attn, ys, gate_a, gate_b, x, wa, wb, wo, fnorm):
    t = x.shape[0]
    tm = _row_tile(t)
    row = lambda n: pl.BlockSpec((tm, n), lambda i: (i, 0))
    full = lambda a: pl.BlockSpec(a.shape, lambda i: (0, 0))
    return pl.pallas_call(
        _merge_kernel,
        grid=(t // tm,),
        in_specs=[row(D_ATTN), row(D_ATTN), row(D_SSM), row(D_MODEL), row(D_MODEL), row(D_MODEL),
                  full(wa), full(wb), full(wo), full(fnorm)],
        out_specs=row(D_MODEL),
        out_shape=jax.ShapeDtypeStruct((t, D_MODEL), F32),
        compiler_params=_cparams("parallel"),
        name="merge",
    )(o, g_attn, ys, gate_a, gate_b, x, wa, wb, wo, fnorm)


def _prep_weights(norm_g, w_in, conv_w, conv_b, dt_bias, a_log, d_skip, ssm_norm,
                  w_branch_a, w_branch_b, w_out, final_norm):
    sizes = (D_ATTN, D_ATTN, D_ATTN, D_ATTN, D_SSM, D_CONV, N_SSM_HEADS, D_MODEL, D_MODEL)
    pts = np.cumsum((0,) + sizes)
    seg = lambda i: w_in[:, pts[i]:pts[i + 1]]
    pad_heads = lambda v: jnp.pad(v.astype(F32), (0, LANES - N_SSM_HEADS)).reshape(1, LANES)
    w_a = w_in[:, 0:pts[4]].astype(BF16)
    w_b = jnp.concatenate([seg(4), seg(7), seg(8)], axis=1).astype(BF16)
    w_c = jnp.concatenate([seg(5), jnp.pad(seg(6), ((0, 0), (0, LANES - N_SSM_HEADS)))], axis=1).astype(BF16)
    tri = jnp.asarray(np.tril(np.ones((CHUNK, CHUNK), np.float32)))
    head_of = np.arange(D_SSM) // SSM_HEAD_DIM
    rep = (np.arange(LANES)[:, None] == head_of[None, :]).astype(np.float32)
    rep2 = jnp.asarray(np.concatenate([rep, rep], axis=0)).astype(BF16)
    hmask = jnp.asarray(np.broadcast_to(
        (np.arange(HEADS_PER_GROUP)[:, None, None] == head_of[None, None, :GROUP_WIDTH]),
        (HEADS_PER_GROUP, CHUNK, GROUP_WIDTH)).astype(np.float32)).astype(BF16)
    conv_wts = (conv_w.astype(F32), conv_b.astype(F32).reshape(1, D_CONV))
    ssd_w = (pad_heads(dt_bias), pad_heads(-jnp.exp(a_log.astype(F32))),
             jnp.repeat(d_skip.astype(F32), SSM_HEAD_DIM).reshape(1, D_SSM),
             ssm_norm.astype(F32).reshape(1, D_SSM), tri, rep2, hmask)
    return dict(norm_g=norm_g.astype(F32).reshape(1, D_MODEL), w_a=w_a, w_b=w_b, w_c=w_c,
                conv_wts=conv_wts, ssd_w=ssd_w,
                wa=w_branch_a.astype(BF16), wb=w_branch_b.astype(BF16),
                wo=w_out.astype(BF16), fnorm=final_norm.astype(F32).reshape(1, D_MODEL))


def _layer(x, cache_k, cache_v, conv_past, ssm_past, wts, rel_bias):
    bsz, seq, _ = x.shape
    t = bsz * seq
    x2 = x.reshape(t, D_MODEL)
    h, q, k, v, g_attn, *kv_t = _proj_a(x2, wts["norm_g"], wts["w_a"], bsz, seq)
    zs, gate_a, gate_b = _proj(h, wts["w_b"], (D_SSM, D_MODEL, D_MODEL), "proj_b", silu_first=True)
    h0 = None if ssm_past is None else ssm_past.astype(F32).reshape(bsz, D_SSM, D_STATE)

    if cache_k is None:
        o = _attn_prompt(q, k, v, _window_bias(rel_bias), bsz, seq)
        past8 = jnp.zeros((bsz, SUBLANES, D_CONV), F32)
        xs, bm, cm, dt, tail = _proj_conv(h, wts["w_c"], *wts["conv_wts"], past8, bsz, seq)
        ys, ssm_new = _ssd(xs, bm, cm, dt, zs, None, h0, None, wts["ssd_w"], bsz, seq)
    else:
        buf = cache_k.shape[1]
        feature_major = lambda a: jnp.transpose(a, (0, 2, 3, 1)).reshape(bsz, D_ATTN, buf)
        o = _attn_sample(q, k, v, feature_major(cache_k), feature_major(cache_v),
                         _sample_bias(rel_bias, seq, buf), bsz, seq)
        past8 = jnp.pad(conv_past.astype(F32), ((0, 0), (SUBLANES - (CONV_WIDTH - 1), 0), (0, 0)))
        xs, bm, cm, dt = _proj(h, wts["w_c"], (D_SSM, D_BC, D_BC, LANES), "proj_c")
        ys, ssm_new, tail = _ssd(xs, bm, cm, dt, zs, past8, h0, wts["conv_wts"], wts["ssd_w"], bsz, seq)
    y = _merge(o, g_attn, ys, gate_a, gate_b, x2, wts["wa"], wts["wb"], wts["wo"], wts["fnorm"])
    conv_new = tail[:, SUBLANES - (CONV_WIDTH - 1):]
    if kv_t:
        per_head = lambda a: jnp.transpose(a.reshape(bsz, N_HEADS_A, HEAD_DIM_A, seq), (0, 3, 1, 2))
        k_out, v_out = per_head(kv_t[0]), per_head(kv_t[1])
    else:
        k_out = k.reshape(bsz, seq, N_HEADS_A, HEAD_DIM_A)
        v_out = v.reshape(bsz, seq, N_HEADS_A, HEAD_DIM_A)
    return (y.reshape(bsz, seq, D_MODEL), k_out, v_out, conv_new,
            ssm_new.reshape(bsz, N_SSM_HEADS, SSM_HEAD_DIM, D_STATE))


def kernel(x_prompt, x_sample, cache_k, cache_v, state_conv, state_ssm, norm_g, w_in, conv_w, conv_b,
           dt_bias, a_log, d_skip, ssm_norm, w_branch_a, w_branch_b, w_out, rel_bias, final_norm):
    assert w_in.shape[0] == 1, "single layer"
    wts = _prep_weights(norm_g[0], w_in[0], conv_w[0], conv_b[0], dt_bias[0], a_log[0], d_skip[0],
                        ssm_norm[0], w_branch_a[0], w_branch_b[0], w_out[0], final_norm)
    keep = min(WINDOW_MAX, x_prompt.shape[1])
    yp, kp, vp, cp, sp = _layer(x_prompt, None, None, None, None, wts, rel_bias)
    ys, ks, vs, cs, ss = _layer(x_sample, cache_k[0], cache_v[0], state_conv[0], state_ssm[0], wts, rel_bias)
    return (yp, ys, kp[:, -keep:][None], vp[:, -keep:][None], cp[None], sp[None],
            ks[None], vs[None], cs[None], ss[None])
```

```python
import functools
import math

import numpy as np
import jax
import jax.numpy as jnp
from jax import lax
from jax.experimental import pallas as pl
from jax.experimental.pallas import tpu as pltpu

F32 = jnp.float32
BF16 = jnp.bfloat16

D_MODEL = 1024
N_HEADS_A = 12
HEAD_DIM_A = 64
D_ATTN = N_HEADS_A * HEAD_DIM_A
DILATIONS = (1, 4, 16)
WINDOW_KEYS = 128
WINDOW_MAX = 2048
N_BUCKETS = 32
MAX_DISTANCE = WINDOW_MAX
D_SSM = 2048
SSM_HEAD_DIM = 64
N_SSM_HEADS = D_SSM // SSM_HEAD_DIM
SSM_GROUPS = 8
HEADS_PER_GROUP = N_SSM_HEADS // SSM_GROUPS
GROUP_WIDTH = D_SSM // SSM_GROUPS
D_STATE = 128
CONV_WIDTH = 4
D_BC = SSM_GROUPS * D_STATE
D_CONV = D_SSM + 2 * D_BC
CHUNK = 128
LANES = 128
SUBLANES = 8
NEG = -1e30
BLOCKS_PER_STEP = 8
LOG2E = math.log2(math.e)
VMEM_LIMIT = 48 * 1024 * 1024
PROJ_CONV_VMEM_LIMIT = 56 * 1024 * 1024


def _cparams(*sem):
    return pltpu.CompilerParams(dimension_semantics=sem, vmem_limit_bytes=VMEM_LIMIT)


def _dot(a, b):
    return jnp.dot(a, b, preferred_element_type=F32)


def _dot_nt(a, b):
    return lax.dot_general(a, b, (((1,), (1,)), ((), ())), preferred_element_type=F32)


def _silu(x):
    return x * (1.0 / (1.0 + jnp.exp(-x)))


def _sigmoid(x):
    return 1.0 / (1.0 + jnp.exp(-x))


def _div_pow2(x, n):
    assert n & (n - 1) == 0
    return jnp.right_shift(x, int(math.log2(n)))


def _proj_a_kernel(x_ref, g_ref, w_ref, h_ref, q_ref, k_ref, v_ref, ga_ref, *t_refs):
    xf = x_ref[...]
    ms = jnp.mean(xf * xf, axis=-1, keepdims=True)
    h = (xf * lax.rsqrt(ms + 1e-6) * g_ref[...]).astype(BF16)
    h_ref[...] = h
    q_ref[...] = _dot(h, w_ref[:, 0:D_ATTN]) * (LOG2E / math.sqrt(HEAD_DIM_A))
    k = _dot(h, w_ref[:, D_ATTN:2 * D_ATTN])
    k_ref[...] = k
    v = _dot(h, w_ref[:, 2 * D_ATTN:3 * D_ATTN])
    v_ref[...] = v
    ga_ref[...] = _dot(h, w_ref[:, 3 * D_ATTN:4 * D_ATTN])
    if t_refs:
        kt_ref, vt_ref = t_refs
        kt_ref[0] = k.T
        vt_ref[0] = v.T


def _proj_kernel(h_ref, w_ref, *out_refs, widths, silu_first):
    h = h_ref[...]
    c0 = 0
    for o_ref, wd in zip(out_refs, widths):
        val = _dot(h, w_ref[:, c0:c0 + wd])
        o_ref[...] = _silu(val) if silu_first and c0 == 0 else val
        c0 += wd


def _row_tile(t):
    return 512 if t % 512 == 0 else t


def _proj_a(x, norm_g, w_a, bsz, seq):
    t = x.shape[0]
    tm = _row_tile(t)
    row = lambda n: pl.BlockSpec((tm, n), lambda i: (i, 0))
    full = lambda a: pl.BlockSpec(a.shape, lambda i: (0, 0))
    outs = [jax.ShapeDtypeStruct((t, D_MODEL), BF16)] + [jax.ShapeDtypeStruct((t, D_ATTN), F32)] * 4
    out_specs = [row(s.shape[1]) for s in outs]
    if seq % tm == 0:
        per_seq = seq // tm
        outs += [jax.ShapeDtypeStruct((bsz, D_ATTN, seq), F32)] * 2
        out_specs += [pl.BlockSpec((1, D_ATTN, tm), lambda i: (i // per_seq, 0, i % per_seq))] * 2
    return pl.pallas_call(
        _proj_a_kernel,
        grid=(t // tm,),
        in_specs=[row(D_MODEL), full(norm_g), full(w_a)],
        out_specs=out_specs,
        out_shape=outs,
        compiler_params=_cparams("parallel"),
        name="proj_a",
    )(x, norm_g, w_a)


def _proj(h, w, widths, name, silu_first=False):
    t = h.shape[0]
    tm = _row_tile(t)
    row = lambda n: pl.BlockSpec((tm, n), lambda i: (i, 0))
    outs = [jax.ShapeDtypeStruct((t, wd), F32) for wd in widths]
    return pl.pallas_call(
        functools.partial(_proj_kernel, widths=widths, silu_first=silu_first),
        grid=(t // tm,),
        in_specs=[row(D_MODEL), pl.BlockSpec(w.shape, lambda i: (0, 0))],
        out_specs=[row(wd) for wd in widths],
        out_shape=outs,
        compiler_params=_cparams("parallel"),
        name=name,
    )(h, w)


def _t5_bucket(dist):
    max_exact = N_BUCKETS // 2
    d = np.maximum(dist, 1).astype(np.float32)
    large = max_exact + (np.log(d / max_exact) / math.log(MAX_DISTANCE / max_exact)
                         * (N_BUCKETS - max_exact)).astype(np.int32)
    large = np.minimum(large, N_BUCKETS - 1)
    return np.where(dist < max_exact, dist, large).astype(np.int32)


def _bias_by_distance(rel_bias, dist, valid, extra=None):
    onehot = (_t5_bucket(np.maximum(dist, 0))[:, None] == np.arange(N_BUCKETS)[None, :]) & valid[:, None]
    b = jnp.dot(jnp.asarray(onehot.astype(np.float32)), rel_bias.astype(F32), precision=lax.Precision.HIGHEST)
    if extra is not None:
        b = b + jnp.asarray(extra.astype(np.float32))[:, None]
    return jnp.where(jnp.asarray(valid)[:, None], b * LOG2E, NEG)


def _window_bias(rel_bias):
    u = np.arange(2 * CHUNK)
    has_prev = jnp.asarray(np.arange(2 * CHUNK) >= CHUNK)
    out = []
    for d in DILATIONS:
        g = _bias_by_distance(rel_bias, (WINDOW_KEYS - u) * d, u <= WINDOW_KEYS).T
        reps = jnp.tile(g, (1, CHUNK * 2))[:, :CHUNK * (4 * CHUNK - 1)]
        toep = reps.reshape(N_HEADS_A, CHUNK, 4 * CHUNK - 1)[:, :, :2 * CHUNK]
        toep = toep.reshape(N_HEADS_A // 2, 2 * CHUNK, 2 * CHUNK)
        out.append(jnp.stack([toep, jnp.where(has_prev, toep, NEG)], axis=1))
    return jnp.stack(out)


def _sample_bias(rel_bias, lq, buf):
    ncol = buf + CHUNK
    dist = np.arange(buf + lq - 1, -(CHUNK + lq) - 1, -1)
    mult = np.zeros(dist.shape, np.int32)
    for d in DILATIONS:
        mult += ((dist >= 0) & (dist % d == 0) & (dist <= WINDOW_KEYS * d)).astype(np.int32)
    fr = _bias_by_distance(rel_bias, dist, mult > 0, np.log(np.maximum(mult, 1))).T
    rows = [fr[:, lq - 1 - i:lq - 1 - i + ncol] for i in range(lq)]
    return jnp.stack(rows, axis=1).reshape(N_HEADS_A * lq, ncol)


def _attn_prompt_kernel(q_ref, k_ref, v_ref, bias_ref, o_ref, m_ref, l_ref,
                        q4_ref, k4_ref, v4_ref, o4_ref, m4_ref, l4_ref, *, seq):
    s1 = DILATIONS[1]
    assert DILATIONS == (1, s1, s1 * s1) and seq // DILATIONS[2] == CHUNK
    lc4 = seq // s1
    lane = lax.broadcasted_iota(jnp.int32, (CHUNK, LANES), 1)
    first_half = lane < HEAD_DIM_A

    def rows(ref, start, stride):
        if stride == 1:
            return ref[0, pl.ds(start, CHUNK), :]
        return ref[0, pl.ds(start, CHUNK, stride=stride), :]

    def put(ref, start, stride, val):
        if stride == 1:
            ref[0, pl.ds(start, CHUNK), :] = val
        else:
            ref[0, pl.ds(start, CHUNK, stride=stride), :] = val

    def pair_tile(x):
        return jnp.where(first_half, x[:CHUNK], x[CHUNK:])

    def scores(gi, src, stride, start, prev_start, first):
        qr, kr, vr = src
        qp = rows(qr, start, stride)
        qs = jnp.concatenate([jnp.where(first_half, qp, 0.0),
                              jnp.where(first_half, 0.0, qp)], axis=0).astype(BF16)
        kw = rows(kr, start, stride).astype(BF16)
        vw = rows(vr, start, stride).astype(BF16)
        if prev_start is None:
            s = _dot_nt(qs, kw) + bias_ref[gi, 0, 0, :, CHUNK:]
        else:
            kw = jnp.concatenate([rows(kr, prev_start, stride).astype(BF16), kw], axis=0)
            vw = jnp.concatenate([rows(vr, prev_start, stride).astype(BF16), vw], axis=0)
            s = _dot_nt(qs, kw) + bias_ref[gi, 0, first]
        m = jnp.max(s, axis=1, keepdims=True)
        e = jnp.exp2(s - m)
        l = pair_tile(jnp.sum(e, axis=1, keepdims=True))
        pv = pair_tile(_dot(e.astype(BF16), vw))
        return pv, pair_tile(m), l

    def accumulate(acc, stride, start, pv, m, l, fresh, final):
        o_acc, m_acc, l_acc = acc
        if not fresh:
            m_old = rows(m_acc, start, stride)
            m_new = jnp.maximum(m_old, m)
            a = jnp.exp2(m_old - m_new)
            b = jnp.exp2(m - m_new)
            pv = a * rows(o_acc, start, stride) + b * pv
            l = a * rows(l_acc, start, stride) + b * l
            m = m_new
        if final:
            put(o_acc, start, stride, pv * (1.0 / l))
        else:
            put(o_acc, start, stride, pv)
            put(m_acc, start, stride, m)
            put(l_acc, start, stride, l)

    def group(gi, src, acc, stride, n_blocks, locate, fresh=False, final=False):
        def body(it, carry):
            starts, parts = [], []
            for u in range(BLOCKS_PER_STEP):
                start, prev, first = locate(it * BLOCKS_PER_STEP + u)
                starts.append(start)
                parts.append(scores(gi, src, stride, start, prev, first))
            for start, part in zip(starts, parts):
                accumulate(acc, stride, start, *part, fresh, final)
            return carry

        lax.fori_loop(0, n_blocks // BLOCKS_PER_STEP, body, 0)

    def windowed(blk, blocks_per_class):
        i = blk % blocks_per_class
        start = pl.multiple_of(blk * CHUNK, CHUNK)
        prev = pl.multiple_of((blk - jnp.minimum(i, 1)) * CHUNK, CHUNK)
        return start, prev, jnp.where(i == 0, 1, 0)

    def class16(blk):
        return (blk % s1) * lc4 + blk // s1, None, None

    for src, dst in ((q_ref, q4_ref), (k_ref, k4_ref), (v_ref, v4_ref)):
        for r in range(s1):
            dst[0, r * lc4:(r + 1) * lc4, :] = src[0, pl.ds(r, lc4, stride=s1), :]

    token_order = (q_ref, k_ref, v_ref), (o_ref, m_ref, l_ref)
    class_order = (q4_ref, k4_ref, v4_ref), (o4_ref, m4_ref, l4_ref)
    n_blocks = seq // CHUNK
    group(1, *class_order, 1, n_blocks, lambda blk: windowed(blk, lc4 // CHUNK), fresh=True)
    group(2, *class_order, s1, n_blocks, class16)
    for src, dst in zip(class_order[1], token_order[1]):
        for r in range(s1):
            dst[0, pl.ds(r, lc4, stride=s1), :] = src[0, r * lc4:(r + 1) * lc4, :]
    group(0, *token_order, 1, n_blocks, lambda blk: windowed(blk, n_blocks), final=True)


def _attn_prompt(q, k, v, bias, bsz, seq):
    assert all(seq % (d * CHUNK) == 0 for d in DILATIONS)
    pairs = N_HEADS_A // 2
    view = lambda a: a.reshape(bsz, seq, D_ATTN)
    blk = pl.BlockSpec((1, seq, LANES), lambda b, p: (b, 0, p))
    o = pl.pallas_call(
        functools.partial(_attn_prompt_kernel, seq=seq),
        grid=(bsz, pairs),
        in_specs=[blk, blk, blk,
                  pl.BlockSpec((len(DILATIONS), 1, 2, 2 * CHUNK, 2 * CHUNK), lambda b, p: (0, p, 0, 0, 0))],
        out_specs=blk,
        out_shape=jax.ShapeDtypeStruct((bsz, seq, D_ATTN), F32),
        scratch_shapes=[pltpu.VMEM((1, seq, LANES), F32)] * 8,
        compiler_params=_cparams("parallel", "parallel"),
        name="attn_prompt",
    )(view(q), view(k), view(v), bias)
    return o.reshape(bsz * seq, D_ATTN)


def _attn_sample_kernel(q_ref, kn_ref, vn_ref, ck_ref, cv_ref, bias_ref, o_ref, *, lq, buf):
    nrow = N_HEADS_A * lq
    pad = jnp.zeros((CHUNK - lq, D_ATTN), F32)
    row_head = _div_pow2(lax.broadcasted_iota(jnp.int32, (nrow, D_ATTN), 0), lq)
    col_head = _div_pow2(lax.broadcasted_iota(jnp.int32, (nrow, D_ATTN), 1), HEAD_DIM_A)
    own = row_head == col_head
    qs = jnp.where(own, jnp.concatenate([q_ref[0]] * N_HEADS_A, axis=0), 0.0).astype(BF16)
    kn = jnp.concatenate([kn_ref[0], pad], axis=0).astype(BF16)
    vn = jnp.concatenate([vn_ref[0], pad], axis=0).astype(BF16)
    s_c = _dot(qs, ck_ref[0].astype(BF16)) + bias_ref[:, 0:buf]
    s_n = _dot_nt(qs, kn) + bias_ref[:, buf:buf + CHUNK]
    m = jnp.maximum(jnp.max(s_c, axis=1, keepdims=True), jnp.max(s_n, axis=1, keepdims=True))
    e_c = jnp.exp2(s_c - m)
    e_n = jnp.exp2(s_n - m)
    l = jnp.sum(e_c, axis=1, keepdims=True) + jnp.sum(e_n, axis=1, keepdims=True)
    pv = _dot_nt(e_c.astype(BF16), cv_ref[0].astype(BF16)) + _dot(e_n.astype(BF16), vn)
    pv = jnp.where(own, pv * (1.0 / l), 0.0)
    o = pv[0:lq]
    for h in range(1, N_HEADS_A):
        o = o + pv[h * lq:(h + 1) * lq]
    o_ref[0] = o


def _attn_sample(q, k_new, v_new, cache_kt, cache_vt, bias, bsz, lq):
    buf = cache_kt.shape[2]
    assert lq & (lq - 1) == 0 and lq <= CHUNK
    tok = pl.BlockSpec((1, lq, D_ATTN), lambda b: (b, 0, 0))
    cache = pl.BlockSpec((1, D_ATTN, buf), lambda b: (b, 0, 0))
    per_seq = lambda a: a.reshape(bsz, lq, D_ATTN)
    o = pl.pallas_call(
        functools.partial(_attn_sample_kernel, lq=lq, buf=buf),
        grid=(bsz,),
        in_specs=[tok, tok, tok, cache, cache, pl.BlockSpec(bias.shape, lambda b: (0, 0))],
        out_specs=tok,
        out_shape=jax.ShapeDtypeStruct((bsz, lq, D_ATTN), F32),
        compiler_params=_cparams("parallel"),
        name="attn_sample",
    )(per_seq(q), per_seq(k_new), per_seq(v_new), cache_kt, cache_vt, bias)
    return o.reshape(bsz * lq, D_ATTN)


def _conv_silu(ext, cw_ref, cb_ref, cs):
    acc = cb_ref[:, cs] + ext[SUBLANES:] * cw_ref[CONV_WIDTH - 1:CONV_WIDTH, cs]
    for shift in range(1, CONV_WIDTH):
        tap = CONV_WIDTH - 1 - shift
        acc = acc + pltpu.roll(ext, shift, axis=0)[SUBLANES:] * cw_ref[tap:tap + 1, cs]
    return _silu(acc)


def _proj_conv_kernel(h_ref, wb_ref, wc_ref, cw_ref, cb_ref, past_ref,
                      zs_ref, ga_ref, gb_ref, xs_ref, bm_ref, cm_ref, dt_ref, tail_ref,
                      halo_ref, *, tiles_per_seq):
    i = pl.program_id(0)
    tm = h_ref.shape[0]

    @pl.when(i % tiles_per_seq == 0)
    def _():
        halo_ref[...] = past_ref[0]

    h = h_ref[...]
    slab = 512
    for j in range(D_CONV // slab):
        cs = slice(j * slab, (j + 1) * slab)
        pre = _dot(h, wc_ref[:, cs])
        out = _conv_silu(jnp.concatenate([halo_ref[:, cs], pre], axis=0), cw_ref, cb_ref, cs)
        halo_ref[:, cs] = pre[tm - SUBLANES:]
        if cs.stop <= D_SSM:
            xs_ref[:, cs] = out
        elif cs.stop <= D_SSM + D_BC:
            bm_ref[:, cs.start - D_SSM:cs.stop - D_SSM] = out.astype(BF16)
        else:
            cm_ref[:, cs.start - D_SSM - D_BC:cs.stop - D_SSM - D_BC] = out.astype(BF16)
        side = _dot(h, wb_ref[:, cs])
        if cs.stop <= D_SSM:
            zs_ref[:, cs] = _silu(side)
        elif cs.stop <= D_SSM + D_MODEL:
            ga_ref[:, cs.start - D_SSM:cs.stop - D_SSM] = side
        else:
            gb_ref[:, cs.start - D_SSM - D_MODEL:cs.stop - D_SSM - D_MODEL] = side
    dt_ref[...] = _dot(h, wc_ref[:, D_CONV:D_CONV + LANES])

    @pl.when(i % tiles_per_seq == tiles_per_seq - 1)
    def _():
        tail_ref[0] = halo_ref[...]


def _proj_conv(h, w_b, w_c, conv_w, conv_b, past8, bsz, seq):
    t = h.shape[0]
    tm = _row_tile(t)
    assert seq % tm == 0 and w_b.shape[1] == D_CONV == D_SSM + 2 * D_MODEL
    per_seq = seq // tm
    row = lambda n: pl.BlockSpec((tm, n), lambda i: (i, 0))
    full = lambda a: pl.BlockSpec(a.shape, lambda i: (0, 0))
    once = lambda a: pl.BlockSpec(a.shape, lambda i: (0, 0), pipeline_mode=pl.Buffered(1))
    edge = pl.BlockSpec((1, SUBLANES, D_CONV), lambda i: (i // per_seq, 0, 0))
    f32 = lambda n: jax.ShapeDtypeStruct((t, n), F32)
    return pl.pallas_call(
        functools.partial(_proj_conv_kernel, tiles_per_seq=per_seq),
        grid=(t // tm,),
        in_specs=[row(D_MODEL), once(w_b), once(w_c), full(conv_w), full(conv_b), edge],
        out_specs=[row(D_SSM), row(D_MODEL), row(D_MODEL), row(D_SSM), row(D_BC), row(D_BC), row(LANES), edge],
        out_shape=[f32(D_SSM), f32(D_MODEL), f32(D_MODEL), f32(D_SSM),
                   jax.ShapeDtypeStruct((t, D_BC), BF16), jax.ShapeDtypeStruct((t, D_BC), BF16), f32(LANES),
                   jax.ShapeDtypeStruct((bsz, SUBLANES, D_CONV), F32)],
        scratch_shapes=[pltpu.VMEM((SUBLANES, D_CONV), F32)],
        compiler_params=pltpu.CompilerParams(dimension_semantics=("arbitrary",),
                                             vmem_limit_bytes=PROJ_CONV_VMEM_LIMIT),
        name="proj_conv",
    )(h, w_b, w_c, conv_w, conv_b, past8)


def _ssd_kernel(*refs, rows, has_state, conv_inside):
    (xs_ref, bm_ref, cm_ref, dt_ref, z_ref) = refs[:5]
    refs = refs[5:]
    if conv_inside:
        past_ref = refs[0]
        refs = refs[1:]
    if has_state:
        h0_ref = refs[0]
        refs = refs[1:]
    if conv_inside:
        cw_ref, cb_ref = refs[:2]
        refs = refs[2:]
    (dtb_ref, a_ref, dsk_ref, nrm_ref, tri_ref, rep_ref, hmask_ref, y_ref, hout_ref) = refs[:9]
    c = pl.program_id(1)

    @pl.when(c == 0)
    def _():
        if has_state:
            hout_ref[0] = h0_ref[0]
        else:
            hout_ref[0] = jnp.zeros((D_SSM, D_STATE), F32)

    if conv_inside:
        tail_ref, xpad_ref, xc_ref = refs[9:]

        @pl.when(c == 0)
        def _():
            xpad_ref[0:SUBLANES, :] = past_ref[0]
            if rows < CHUNK:
                xpad_ref[SUBLANES + rows:SUBLANES + CHUNK, :] = jnp.zeros((CHUNK - rows, D_CONV), F32)

        @pl.when(c > 0)
        def _():
            xpad_ref[0:SUBLANES, :] = xpad_ref[CHUNK:CHUNK + SUBLANES, :]

        xpad_ref[SUBLANES:SUBLANES + rows, 0:D_SSM] = xs_ref[0]
        xpad_ref[SUBLANES:SUBLANES + rows, D_SSM:D_SSM + D_BC] = bm_ref[0]
        xpad_ref[SUBLANES:SUBLANES + rows, D_SSM + D_BC:D_CONV] = cm_ref[0]

        @pl.when(c == pl.num_programs(1) - 1)
        def _():
            tail_ref[0] = xpad_ref[rows:rows + SUBLANES, :]

        slab = 512
        for j in range(D_CONV // slab):
            cs = slice(j * slab, (j + 1) * slab)
            xc_ref[:, cs] = _conv_silu(xpad_ref[:, cs], cw_ref, cb_ref, cs)
        x_of = lambda gs: xc_ref[:, gs]
        b_of = lambda g: xc_ref[:, D_SSM + g * D_STATE:D_SSM + (g + 1) * D_STATE].astype(BF16)
        c_of = lambda g: xc_ref[:, D_SSM + D_BC + g * D_STATE:D_SSM + D_BC + (g + 1) * D_STATE].astype(BF16)
    else:
        assert rows == CHUNK
        x_of = lambda gs: xs_ref[0, :, gs]
        b_of = lambda g: bm_ref[0, :, g * D_STATE:(g + 1) * D_STATE]
        c_of = lambda g: cm_ref[0, :, g * D_STATE:(g + 1) * D_STATE]

    dt_raw = dt_ref[0]
    if rows < CHUNK:
        dt_raw = jnp.concatenate([dt_raw, jnp.zeros((CHUNK - rows, LANES), F32)], axis=0)
    pre = dt_raw + dtb_ref[...]
    dt = jnp.maximum(pre, 0.0) + jnp.log(1.0 + jnp.exp(-jnp.abs(pre)))
    if rows < CHUNK:
        rid = lax.broadcasted_iota(jnp.int32, (CHUNK, LANES), 0)
        dt = jnp.where(rid < rows, dt, 0.0)
    da = dt * a_ref[...]
    acum = jnp.dot(tri_ref[...], da, preferred_element_type=F32, precision=lax.Precision.HIGHEST)
    acum_t = acum.T
    last = acum[CHUNK - 1:CHUNK, :]
    e_acum = jnp.exp(acum)
    w_coef = dt * jnp.exp(last - acum)

    ri = lax.broadcasted_iota(jnp.int32, (CHUNK, CHUNK), 0)
    ci = lax.broadcasted_iota(jnp.int32, (CHUNK, CHUNK), 1)
    causal = ri >= ci
    chunk_decay = jnp.exp(last)

    coef = jnp.concatenate([dt, e_acum, w_coef], axis=0)
    hi = coef.astype(BF16)
    lo = (coef - hi.astype(F32)).astype(BF16)
    wide = _dot(jnp.concatenate([hi, lo], axis=1), rep_ref[...])
    dt_w, e_w, w_w = wide[0:CHUNK], wide[CHUNK:2 * CHUNK], wide[2 * CHUNK:3 * CHUNK]

    for g in range(SSM_GROUPS):
        gs = slice(g * GROUP_WIDTH, (g + 1) * GROUP_WIDTH)
        bg = b_of(g)
        cg = c_of(g)
        xg = x_of(gs)
        cb = _dot_nt(cg, bg)
        xdt = (xg * dt_w[:, gs]).astype(BF16)
        m_parts, x_parts = [], []
        for k in range(HEADS_PER_GROUP):
            h = HEADS_PER_GROUP * g + k
            seg = acum[:, h:h + 1] - acum_t[h:h + 1, :]
            decay = jnp.exp(jnp.where(causal, seg, NEG))
            m_parts.append((cb * decay).astype(BF16))
            x_parts.append(xdt * hmask_ref[k])
        y = _dot(jnp.concatenate(m_parts, axis=1), jnp.concatenate(x_parts, axis=0))
        hg = hout_ref[0, gs, :]
        y = y + _dot_nt(cg, hg.astype(BF16)) * e_w[:, gs]
        y = y + dsk_ref[:, gs] * xg
        upd = _dot((xg * w_w[:, gs]).T.astype(BF16), bg)
        for k in range(HEADS_PER_GROUP):
            h = HEADS_PER_GROUP * g + k
            hs = slice(k * SSM_HEAD_DIM, (k + 1) * SSM_HEAD_DIM)
            hout_ref[0, h * SSM_HEAD_DIM:(h + 1) * SSM_HEAD_DIM, :] = chunk_decay[:, h:h + 1] * hg[hs] + upd[hs]
        yz = y[:rows] * z_ref[0, :, gs]
        ms = jnp.mean(yz * yz, axis=1, keepdims=True)
        y_ref[0, :, gs] = (yz * lax.rsqrt(ms + 1e-5) * nrm_ref[:, gs]).astype(y_ref.dtype)


def _ssd(xs, bm, cm, dt, zs, past8, h0, conv_wts, wts, bsz, seq):
    rows = CHUNK if seq % CHUNK == 0 else seq
    nc = seq // rows
    tok = lambda n: pl.BlockSpec((1, rows, n), lambda b, c: (b * nc + c, 0, 0))
    full2 = lambda a: pl.BlockSpec(a.shape, lambda b, c: (0,) * a.ndim)
    state = pl.BlockSpec((1, D_SSM, D_STATE), lambda b, c: (b, 0, 0))
    edge = pl.BlockSpec((1, SUBLANES, D_CONV), lambda b, c: (b, 0, 0))
    has_state = h0 is not None
    conv_inside = past8 is not None
    chunks = lambda a: a.reshape(bsz * nc, rows, a.shape[-1])
    ins = ([chunks(a) for a in (xs, bm, cm, dt, zs)] + ([past8] if conv_inside else [])
           + ([h0] if has_state else []) + (list(conv_wts) if conv_inside else []) + list(wts))
    in_specs = ([tok(D_SSM), tok(D_BC), tok(D_BC), tok(LANES), tok(D_SSM)] + ([edge] if conv_inside else [])
                + ([state] if has_state else [])
                + [full2(w) for w in (list(conv_wts) if conv_inside else []) + list(wts)])
    out_specs = [tok(D_SSM), state]
    out_shape = [jax.ShapeDtypeStruct((bsz * nc, rows, D_SSM), BF16),
                 jax.ShapeDtypeStruct((bsz, D_SSM, D_STATE), F32)]
    scratch = []
    if conv_inside:
        out_specs.append(edge)
        out_shape.append(jax.ShapeDtypeStruct((bsz, SUBLANES, D_CONV), F32))
        scratch = [pltpu.VMEM((SUBLANES + CHUNK, D_CONV), F32), pltpu.VMEM((CHUNK, D_CONV), F32)]
    y, h_new, *tail = pl.pallas_call(
        functools.partial(_ssd_kernel, rows=rows, has_state=has_state, conv_inside=conv_inside),
        grid=(bsz, nc),
        in_specs=in_specs,
        out_specs=out_specs,
        out_shape=out_shape,
        scratch_shapes=scratch,
        compiler_params=_cparams("parallel", "arbitrary"),
        name="ssd",
    )(*ins)
    return (y.reshape(bsz * seq, D_SSM), h_new) + tuple(tail)


def _merge_kernel(o_ref, g_ref, ys_ref, ga_ref, gb_ref, x_ref, wa_ref, wb_ref, wo_ref, fn_ref, y_ref):
    ya = (o_ref[...] * _silu(g_ref[...])).astype(BF16)
    merged = (_sigmoid(ga_ref[...]) * _dot(ya, wa_ref[...])
              + _sigmoid(gb_ref[...]) * _dot(ys_ref[...], wb_ref[...]))
    out = x_ref[...] + _dot(merged.astype(BF16), wo_ref[...])
    ms = jnp.mean(out * out, axis=-1, keepdims=True)
    y_ref[...] = out * lax.rsqrt(ms + 1e-6) * fn_ref[...]


def _merge(o, g_attn, ys, gate_a, gate_b, x, wa, wb, wo, fnorm):
    t = x.shape[0]
    tm = _row_tile(t)
    row = lambda n: pl.BlockSpec((tm, n), lambda i: (i, 0))
    full = lambda a: pl.BlockSpec(a.shape, lambda i: (0, 0))
    return pl.pallas_call(
        _merge_kernel,
        grid=(t // tm,),
        in_specs=[row(D_ATTN), row(D_ATTN), row(D_SSM), row(D_MODEL), row(D_MODEL), row(D_MODEL),
                  full(wa), full(wb), full(wo), full(fnorm)],
        out_specs=row(D_MODEL),
        out_shape=jax.ShapeDtypeStruct((t, D_MODEL), F32),
        compiler_params=_cparams("parallel"),
        name="merge",
    )(o, g_attn, ys, gate_a, gate_b, x, wa, wb, wo, fnorm)


def _prep_weights(norm_g, w_in, conv_w, conv_b, dt_bias, a_log, d_skip, ssm_norm,
                  w_branch_a, w_branch_b, w_out, final_norm):
    sizes = (D_ATTN, D_ATTN, D_ATTN, D_ATTN, D_SSM, D_CONV, N_SSM_HEADS, D_MODEL, D_MODEL)
    pts = np.cumsum((0,) + sizes)
    seg = lambda i: w_in[:, pts[i]:pts[i + 1]]
    pad_heads = lambda v: jnp.pad(v.astype(F32), (0, LANES - N_SSM_HEADS)).reshape(1, LANES)
    w_a = w_in[:, 0:pts[4]].astype(BF16)
    w_b = jnp.concatenate([seg(4), seg(7), seg(8)], axis=1).astype(BF16)
    w_c = jnp.concatenate([seg(5), jnp.pad(seg(6), ((0, 0), (0, LANES - N_SSM_HEADS)))], axis=1).astype(BF16)
    tri = jnp.asarray(np.tril(np.ones((CHUNK, CHUNK), np.float32)))
    head_of = np.arange(D_SSM) // SSM_HEAD_DIM
    rep = (np.arange(LANES)[:, None] == head_of[None, :]).astype(np.float32)
    rep2 = jnp.asarray(np.concatenate([rep, rep], axis=0)).astype(BF16)
    hmask = jnp.asarray(np.broadcast_to(
        (np.arange(HEADS_PER_GROUP)[:, None, None] == head_of[None, None, :GROUP_WIDTH]),
        (HEADS_PER_GROUP, CHUNK, GROUP_WIDTH)).astype(np.float32)).astype(BF16)
    conv_wts = (conv_w.astype(F32), conv_b.astype(F32).reshape(1, D_CONV))
    ssd_w = (pad_heads(dt_bias), pad_heads(-jnp.exp(a_log.astype(F32))),
             jnp.repeat(d_skip.astype(F32), SSM_HEAD_DIM).reshape(1, D_SSM),
             ssm_norm.astype(F32).reshape(1, D_SSM), tri, rep2, hmask)
    return dict(norm_g=norm_g.astype(F32).reshape(1, D_MODEL), w_a=w_a, w_b=w_b, w_c=w_c,
                conv_wts=conv_wts, ssd_w=ssd_w,
                wa=w_branch_a.astype(BF16), wb=w_branch_b.astype(BF16),
                wo=w_out.astype(BF16), fnorm=final_norm.astype(F32).reshape(1, D_MODEL))


def _layer(x, cache_k, cache_v, conv_past, ssm_past, wts, rel_bias):
    bsz, seq, _ = x.shape
    t = bsz * seq
    x2 = x.reshape(t, D_MODEL)
    h, q, k, v, g_attn, *kv_t = _proj_a(x2, wts["norm_g"], wts["w_a"], bsz, seq)
    h0 = None if ssm_past is None else ssm_past.astype(F32).reshape(bsz, D_SSM, D_STATE)

    if cache_k is None:
        o = _attn_prompt(q, k, v, _window_bias(rel_bias), bsz, seq)
        past8 = jnp.zeros((bsz, SUBLANES, D_CONV), F32)
        zs, gate_a, gate_b, xs, bm, cm, dt, tail = _proj_conv(h, wts["w_b"], wts["w_c"], *wts["conv_wts"],
                                                              past8, bsz, seq)
        ys, ssm_new = _ssd(xs, bm, cm, dt, zs, None, h0, None, wts["ssd_w"], bsz, seq)
    else:
        zs, gate_a, gate_b = _proj(h, wts["w_b"], (D_SSM, D_MODEL, D_MODEL), "proj_b", silu_first=True)
        buf = cache_k.shape[1]
        feature_major = lambda a: jnp.transpose(a, (0, 2, 3, 1)).reshape(bsz, D_ATTN, buf)
        o = _attn_sample(q, k, v, feature_major(cache_k), feature_major(cache_v),
                         _sample_bias(rel_bias, seq, buf), bsz, seq)
        past8 = jnp.pad(conv_past.astype(F32), ((0, 0), (SUBLANES - (CONV_WIDTH - 1), 0), (0, 0)))
        xs, bm, cm, dt = _proj(h, wts["w_c"], (D_SSM, D_BC, D_BC, LANES), "proj_c")
        ys, ssm_new, tail = _ssd(xs, bm, cm, dt, zs, past8, h0, wts["conv_wts"], wts["ssd_w"], bsz, seq)
    y = _merge(o, g_attn, ys, gate_a, gate_b, x2, wts["wa"], wts["wb"], wts["wo"], wts["fnorm"])
    conv_new = tail[:, SUBLANES - (CONV_WIDTH - 1):]
    if kv_t:
        per_head = lambda a: jnp.transpose(a.reshape(bsz, N_HEADS_A, HEAD_DIM_A, seq), (0, 3, 1, 2))
        k_out, v_out = per_head(kv_t[0]), per_head(kv_t[1])
    else:
        k_out = k.reshape(bsz, seq, N_HEADS_A, HEAD_DIM_A)
        v_out = v.reshape(bsz, seq, N_HEADS_A, HEAD_DIM_A)
    return (y.reshape(bsz, seq, D_MODEL), k_out, v_out, conv_new,
            ssm_new.reshape(bsz, N_SSM_HEADS, SSM_HEAD_DIM, D_STATE))


def kernel(x_prompt, x_sample, cache_k, cache_v, state_conv, state_ssm, norm_g, w_in, conv_w, conv_b,
           dt_bias, a_log, d_skip, ssm_norm, w_branch_a, w_branch_b, w_out, rel_bias, final_norm):
    assert w_in.shape[0] == 1, "single layer"
    wts = _prep_weights(norm_g[0], w_in[0], conv_w[0], conv_b[0], dt_bias[0], a_log[0], d_skip[0],
                        ssm_norm[0], w_branch_a[0], w_branch_b[0], w_out[0], final_norm)
    keep = min(WINDOW_MAX, x_prompt.shape[1])
    yp, kp, vp, cp, sp = _layer(x_prompt, None, None, None, None, wts, rel_bias)
    ys, ks, vs, cs, ss = _layer(x_sample, cache_k[0], cache_v[0], state_conv[0], state_ssm[0], wts, rel_bias)
    return (yp, ys, kp[:, -keep:][None], vp[:, -keep:][None], cp[None], sp[None],
            ks[None], vs[None], cs[None], ss[None])
```
